```python
import math
import jax
import jax.numpy as jnp
from jax import lax
import numpy as np

D_MODEL = 1024
BATCH = 16
SEQ = 4096
DEPTH = 2

GRID_W = 64
CTX_LEN = 256
HEAD_DIM = 64
ATTN_SCALE = HEAD_DIM ** -0.5
ROPE_BASE = 10000.0
Q_BLOCK = 128
NORM_EPS = 1e-6

SSD_HEADS = 8
SSD_HEAD_DIM = 64
SSD_INNER = SSD_HEADS * SSD_HEAD_DIM
SSD_GROUPS = 2
SSD_STATE = 128
SSD_CONV = 4
SSD_CHUNK = 128
SSD_XBC = SSD_INNER + 2 * SSD_GROUPS * SSD_STATE

GQA_HEADS = 8
GQA_KV_HEADS = 2
GQA_REP = GQA_HEADS // GQA_KV_HEADS
GQA_WIDTH = GQA_HEADS * HEAD_DIM
GQA_KV_WIDTH = GQA_KV_HEADS * HEAD_DIM

DIFF_HEADS = 4
DIFF_WIDTH = DIFF_HEADS * 2 * HEAD_DIM

RG_WIDTH = 512
RG_BLOCKS = 8
RG_BLOCK = RG_WIDTH // RG_BLOCKS
RG_CONV = 4
RG_C = 8.0

N_BRANCHES = 4
BRANCH_WIDTH = 512
D_FF = 2816
FFN_CONV = 3

IN_SPLITS = (SSD_INNER, SSD_XBC, 2 * SSD_HEADS,
             GQA_WIDTH, GQA_KV_WIDTH, GQA_KV_WIDTH,
             DIFF_WIDTH, DIFF_WIDTH, DIFF_WIDTH,
             RG_WIDTH, RG_WIDTH)
IN_COLS = 4880

kernel_name = 'hybrid_prefix_dit_ssd_gqa_diff_rglru'


def rmsnorm(x, g):
    xf = x.astype(jnp.float32)
    y = xf * lax.rsqrt(jnp.mean(xf * xf, axis=-1, keepdims=True) + NORM_EPS)
    return (y * g.astype(jnp.float32)).astype(x.dtype)


def modulate(h, shift, scale):
    return h * (1.0 + scale) + shift


def dwconv(x, w, b, left):
    k, ch = w.shape
    y = lax.conv_general_dilated(x, w[:, None, :].astype(x.dtype), (1,), [(left, k - 1 - left)],
                                 dimension_numbers=('NWC', 'WIO', 'NWC'), feature_group_count=ch)
    return y + b.astype(x.dtype)


def split_in(u):
    return jnp.split(u, np.cumsum(IN_SPLITS)[:-1].tolist(), axis=-1)


def axial_rope(row, col):
    n_freq = HEAD_DIM // 4
    inv = ROPE_BASE ** (-jnp.arange(n_freq, dtype=jnp.float32) / n_freq)
    ang = jnp.concatenate([row[:, None] * inv, col[:, None] * inv], axis=-1)
    return jnp.cos(ang), jnp.sin(ang)


def apply_rope(x, cos, sin):
    shape = (1, x.shape[1]) + (1,) * (x.ndim - 3) + (cos.shape[-1],)
    cs = cos.reshape(shape).astype(x.dtype)
    sn = sin.reshape(shape).astype(x.dtype)
    x1, x2 = jnp.split(x, 2, axis=-1)
    return jnp.concatenate([x1 * cs - x2 * sn, x1 * sn + x2 * cs], axis=-1)


def softmax32(s):
    return jax.nn.softmax(s.astype(jnp.float32), axis=-1)


def sweep_query_blocks(fn, q):
    bsz, n = q.shape[:2]
    qb = jnp.moveaxis(q.reshape((bsz, n // Q_BLOCK, Q_BLOCK) + q.shape[2:]), 1, 0)
    ob = jnp.moveaxis(lax.map(fn, qb), 0, 1)
    return ob.reshape((bsz, n) + ob.shape[3:])


def gqa_core(q, k, v):
    s = jnp.einsum('bqgrd,bkgd->bgrqk', q, k) * ATTN_SCALE
    p = softmax32(s).astype(v.dtype)
    return jnp.einsum('bgrqk,bkgd->bqgrd', p, v)


def diff_core(q, k, v, lam):
    s = jnp.einsum('bqhcd,bkhcd->bhcqk', q, k) * ATTN_SCALE
    p = softmax32(s)
    a = (p[:, :, 0] - lam * p[:, :, 1]).astype(v.dtype)
    return jnp.einsum('bhqk,bkhd->bqhd', a, v)


def segsum(a):
    t = a.shape[-1]
    cs = jnp.cumsum(a, axis=-1)
    diff = cs[..., :, None] - cs[..., None, :]
    return jnp.where(jnp.tril(jnp.ones((t, t), dtype=bool)), diff, -jnp.inf)


def ssd_chunked(x, a, bm, cm, h0):
    bsz, n, nh, hp = x.shape
    nc = n // SSD_CHUNK
    x = x.reshape(bsz, nc, SSD_CHUNK, nh, hp)
    bm = bm.reshape(bsz, nc, SSD_CHUNK, nh, -1)
    cm = cm.reshape(bsz, nc, SSD_CHUNK, nh, -1)
    a = a.reshape(bsz, nc, SSD_CHUNK, nh).transpose(0, 3, 1, 2)
    a_cs = jnp.cumsum(a, axis=-1)
    decay_in = jnp.exp(segsum(a))
    scores = jnp.einsum('bclhn,bcshn->bhcls', cm, bm) * decay_in
    y_diag = jnp.einsum('bhcls,bcshp->bclhp', scores, x)
    decay_states = jnp.exp(a_cs[..., -1:] - a_cs)
    states = jnp.einsum('bclhn,bhcl,bclhp->bchpn', bm, decay_states, x)
    states = jnp.concatenate([h0[:, None].astype(states.dtype), states], axis=1)
    chunk_decay = jnp.exp(segsum(jnp.pad(a_cs[..., -1], ((0, 0), (0, 0), (1, 0)))))
    new_states = jnp.einsum('bhzc,bchpn->bzhpn', chunk_decay, states)
    states, final = new_states[:, :-1], new_states[:, -1]
    y_off = jnp.einsum('bclhn,bchpn,bhcl->bclhp', cm, states, jnp.exp(a_cs))
    return (y_diag + y_off).reshape(bsz, n, nh, hp), final


def ssd_sequence(z, xbc, dt_raw, p, h0):
    bsz, n, _ = xbc.shape
    xbc = jax.nn.silu(dwconv(xbc, p['ssd_conv_w'], p['ssd_conv_b'], (SSD_CONV - 1) // 2))
    xs, bm, cm = jnp.split(xbc, [SSD_INNER, SSD_INNER + SSD_GROUPS * SSD_STATE], axis=-1)
    xs = xs.reshape(bsz, n, SSD_HEADS, SSD_HEAD_DIM)
    rep = SSD_HEADS // SSD_GROUPS
    bm = jnp.repeat(bm.reshape(bsz, n, SSD_GROUPS, SSD_STATE), rep, axis=2)
    cm = jnp.repeat(cm.reshape(bsz, n, SSD_GROUPS, SSD_STATE), rep, axis=2)
    dt = jax.nn.softplus(dt_raw.astype(jnp.float32).reshape(bsz, n, 2, SSD_HEADS)
                         + p['ssd_dt_bias'].astype(jnp.float32))
    da = dt * (-jnp.exp(p['ssd_a_log'].astype(jnp.float32)))
    xdt = xs[:, :, None] * dt[..., None].astype(xs.dtype)
    y = xs * p['ssd_d'][:, None].astype(xs.dtype)
    finals = []
    for d in range(2):
        init = jnp.zeros((bsz, SSD_HEADS, SSD_HEAD_DIM, SSD_STATE), jnp.float32) if h0 is None else h0[d]
        xd, ad, bd, cd = xdt[:, :, d], da[:, :, d], bm, cm
        if d == 1:
            xd, ad, bd, cd = xd[:, ::-1], ad[:, ::-1], bd[:, ::-1], cd[:, ::-1]
        yd, fd = ssd_chunked(xd, ad, bd, cd, init)
        if d == 1:
            yd = yd[:, ::-1]
        y = y + yd
        finals.append(fd)
    y = y.reshape(bsz, n, SSD_INNER).astype(z.dtype) * jax.nn.silu(z)
    y = rmsnorm(y.reshape(bsz, n, SSD_GROUPS, SSD_INNER // SSD_GROUPS),
                p['ssd_norm_g'].reshape(SSD_GROUPS, -1)).reshape(bsz, n, SSD_INNER)
    return y, finals[0], finals[1]


def linear_scan(a, b, h0):
    b = b.at[:, 0].add(a[:, 0] * h0)

    def combine(left, right):
        a_l, b_l = left
        a_r, b_r = right
        return a_l * a_r, a_r * b_l + b_r

    _, h = lax.associative_scan(combine, (a, b), axis=1)
    return h


def rglru_sequence(x_in, p, h0):
    bsz, n, _ = x_in.shape
    xr = dwconv(x_in, p['rg_conv_w'], p['rg_conv_b'], (RG_CONV - 1) // 2)
    xb = xr.reshape(bsz, n, RG_BLOCKS, RG_BLOCK)
    h_sum = 0.0
    finals = []
    for d in range(2):
        gate_a = jnp.einsum('blkc,kce->blke', xb, p['rg_wa'][d]).reshape(bsz, n, RG_WIDTH) + p['rg_ba'][d]
        gate_x = jnp.einsum('blkc,kce->blke', xb, p['rg_wx'][d]).reshape(bsz, n, RG_WIDTH) + p['rg_bx'][d]
        log_a = -RG_C * jax.nn.sigmoid(gate_a.astype(jnp.float32)) * jax.nn.softplus(-p['rg_lambda'][d].astype(jnp.float32))
        a = jnp.exp(log_a)
        b = jnp.sqrt(-jnp.expm1(2.0 * log_a)) * (jax.nn.sigmoid(gate_x.astype(jnp.float32)) * xr.astype(jnp.float32))
        init = jnp.zeros((bsz, RG_WIDTH), jnp.float32) if h0 is None else h0[d]
        if d == 1:
            a, b = a[:, ::-1], b[:, ::-1]
        h = linear_scan(a, b, init)
        finals.append(h[:, -1])
        if d == 1:
            h = h[:, ::-1]
        h_sum = h_sum + h
    return h_sum.astype(x_in.dtype), finals[0], finals[1]


def gqa_branch(q_l, k_l, v_l, q_c, k_c, v_c, p, cos, sin, with_ctx_out):
    bsz, n, _ = q_l.shape
    m = q_c.shape[1]

    def heads(t, nh):
        return t.reshape(t.shape[0], t.shape[1], nh, HEAD_DIM)

    kc = rmsnorm(heads(k_c, GQA_KV_HEADS), p['gqa_knorm_g'])
    vc = heads(v_c, GQA_KV_HEADS)
    ql = apply_rope(rmsnorm(heads(q_l, GQA_HEADS), p['gqa_qnorm_g']), cos, sin)
    kl = apply_rope(rmsnorm(heads(k_l, GQA_KV_HEADS), p['gqa_knorm_g']), cos, sin)
    k_all = jnp.concatenate([kc, kl], axis=1)
    v_all = jnp.concatenate([vc, heads(v_l, GQA_KV_HEADS)], axis=1)
    ql = ql.reshape(bsz, n, GQA_KV_HEADS, GQA_REP, HEAD_DIM)
    o_l = sweep_query_blocks(lambda qb: gqa_core(qb, k_all, v_all), ql).reshape(bsz, n, GQA_WIDTH)
    o_c = None
    if with_ctx_out:
        qc = rmsnorm(heads(q_c, GQA_HEADS), p['gqa_qnorm_g']).reshape(bsz, m, GQA_KV_HEADS, GQA_REP, HEAD_DIM)
        o_c = gqa_core(qc, kc, vc).reshape(bsz, m, GQA_WIDTH)
    return o_l, o_c


def diff_branch(q_l, k_l, v_l, q_c, k_c, v_c, p, cos, sin, lambda_init, with_ctx_out):
    bsz, n, _ = q_l.shape
    m = q_c.shape[1]
    lp = p['diff_lambda'].astype(jnp.float32)
    lam = jnp.exp(jnp.sum(lp[0] * lp[1])) - jnp.exp(jnp.sum(lp[2] * lp[3])) + lambda_init

    def qk_heads(t):
        return t.reshape(t.shape[0], t.shape[1], DIFF_HEADS, 2, HEAD_DIM)

    def v_heads(t):
        return t.reshape(t.shape[0], t.shape[1], DIFF_HEADS, 2 * HEAD_DIM)

    def finish(o):
        o = rmsnorm(o, p['diff_subln_g']) * (1.0 - lambda_init)
        return o.reshape(o.shape[0], o.shape[1], DIFF_WIDTH)

    kc = rmsnorm(qk_heads(k_c), p['diff_knorm_g'])
    vc = v_heads(v_c)
    ql = apply_rope(rmsnorm(qk_heads(q_l), p['diff_qnorm_g']), cos, sin)
    kl = apply_rope(rmsnorm(qk_heads(k_l), p['diff_knorm_g']), cos, sin)
    k_all = jnp.concatenate([kc, kl], axis=1)
    v_all = jnp.concatenate([vc, v_heads(v_l)], axis=1)
    o_l = finish(sweep_query_blocks(lambda qb: diff_core(qb, k_all, v_all, lam), ql))
    o_c = None
    if with_ctx_out:
        qc = rmsnorm(qk_heads(q_c), p['diff_qnorm_g'])
        o_c = finish(diff_core(qc, kc, vc, lam))
    return o_l, o_c


def merge_branches(h, outs, p):
    m = 0.0
    for k, o in enumerate(outs):
        gate = jax.nn.sigmoid(h @ p['w_gate'][k] + p['b_gate'][k])
        m = m + gate * (o @ p['w_br'][k])
    return m @ p['w_out']


def token_mixer(h_l, h_c, p, cos, sin, lambda_init, with_ctx_out):
    z_l, xbc_l, dt_l, gq_l, gk_l, gv_l, dq_l, dk_l, dv_l, rgg_l, rgx_l = split_in(h_l @ p['w_in'])
    z_c, xbc_c, dt_c, gq_c, gk_c, gv_c, dq_c, dk_c, dv_c, rgg_c, rgx_c = split_in(h_c @ p['w_in'])
    ssd_c, sf, sb = ssd_sequence(z_c, xbc_c, dt_c, p, None)
    ssd_l, _, _ = ssd_sequence(z_l, xbc_l, dt_l, p, (sf, sb))
    gqa_l, gqa_c = gqa_branch(gq_l, gk_l, gv_l, gq_c, gk_c, gv_c, p, cos, sin, with_ctx_out)
    diff_l, diff_c = diff_branch(dq_l, dk_l, dv_l, dq_c, dk_c, dv_c, p, cos, sin, lambda_init, with_ctx_out)
    rg_c, rf, rb = rglru_sequence(rgx_c, p, None)
    rg_l, _, _ = rglru_sequence(rgx_l, p, (rf, rb))
    y_l = merge_branches(h_l, (ssd_l, gqa_l, diff_l, rg_l * jax.nn.gelu(rgg_l)), p)
    y_c = None
    if with_ctx_out:
        y_c = merge_branches(h_c, (ssd_c, gqa_c, diff_c, rg_c * jax.nn.gelu(rgg_c)), p)
    return y_l, y_c


def conv_ffn(h, p):
    u = dwconv(h @ p['w_up'], p['ffn_conv_w'], p['ffn_conv_b'], (FFN_CONV - 1) // 2)
    g, v = jnp.split(u, 2, axis=-1)
    return (jax.nn.silu(g) * v) @ p['w_down']


def layer(x_l, x_c, c, c_ctx, p, cos, sin, lambda_init, with_ctx_out):
    mod_l = (jax.nn.silu(c) @ p['w_ada'] + p['b_ada'])[:, None, :]
    mod_c = (jax.nn.silu(c_ctx) @ p['w_ada'] + p['b_ada'])[None, None, :]
    sh1_l, sc1_l, g1_l, sh2_l, sc2_l, g2_l = jnp.split(mod_l, 6, axis=-1)
    sh1_c, sc1_c, g1_c, sh2_c, sc2_c, g2_c = jnp.split(mod_c, 6, axis=-1)
    h_l = modulate(rmsnorm(x_l, p['norm1_g']), sh1_l, sc1_l)
    h_c = modulate(rmsnorm(x_c, p['norm1_g']), sh1_c, sc1_c)
    y_l, y_c = token_mixer(h_l, h_c, p, cos, sin, lambda_init, with_ctx_out)
    x_l = x_l + g1_l * y_l
    x_l = x_l + g2_l * conv_ffn(modulate(rmsnorm(x_l, p['norm2_g']), sh2_l, sc2_l), p)
    if not with_ctx_out:
        return x_l, None
    x_c = x_c + g1_c * y_c
    x_c = x_c + g2_c * conv_ffn(modulate(rmsnorm(x_c, p['norm2_g']), sh2_c, sc2_c), p)
    return x_l, x_c


def setup_inputs(seed: int = 0) -> dict:
    key = jax.random.key(seed)
    ks = iter(jax.random.split(key, 48))
    f32 = jnp.float32
    L, D = DEPTH, D_MODEL

    def nrm(shape, scale):
        return jax.random.normal(next(ks), shape, f32) * scale

    def gain(shape):
        return 1.0 + nrm(shape, 0.05)

    x = nrm((BATCH, SEQ, D), 1.0)
    c = nrm((BATCH, D), 1.0)
    ctx = nrm((BATCH, CTX_LEN, D), 1.0)
    c_ctx = nrm((D,), 1.0)
    w_ada = nrm((L, D, 6 * D), 0.3 * D ** -0.5)
    b_ada = nrm((L, 6 * D), 0.02)
    norm1_g = gain((L, D))
    norm2_g = gain((L, D))
    w_in = nrm((L, D, IN_COLS), D ** -0.5)
    ssd_conv_w = nrm((L, SSD_CONV, SSD_XBC), SSD_CONV ** -0.5)
    ssd_conv_b = nrm((L, SSD_XBC), 0.02)
    dt0 = jnp.exp(jax.random.uniform(next(ks), (L, 2, SSD_HEADS), f32, math.log(1e-3), math.log(1e-1)))
    ssd_dt_bias = dt0 + jnp.log(-jnp.expm1(-dt0))
    ssd_a_log = jnp.log(jax.random.uniform(next(ks), (L, 2, SSD_HEADS), f32, 1.0, 16.0))
    ssd_d = gain((L, SSD_HEADS))
    ssd_norm_g = gain((L, SSD_INNER))
    gqa_qnorm_g = gain((L, HEAD_DIM))
    gqa_knorm_g = gain((L, HEAD_DIM))
    diff_qnorm_g = gain((L, HEAD_DIM))
    diff_knorm_g = gain((L, HEAD_DIM))
    diff_lambda = nrm((L, 4, HEAD_DIM), 0.1)
    diff_subln_g = gain((L, 2 * HEAD_DIM))
    rg_conv_w = nrm((L, RG_CONV, RG_WIDTH), RG_CONV ** -0.5)
    rg_conv_b = nrm((L, RG_WIDTH), 0.02)
    rg_wa = nrm((L, 2, RG_BLOCKS, RG_BLOCK, RG_BLOCK), RG_BLOCK ** -0.5)
    rg_ba = nrm((L, 2, RG_WIDTH), 0.02)
    rg_wx = nrm((L, 2, RG_BLOCKS, RG_BLOCK, RG_BLOCK), RG_BLOCK ** -0.5)
    rg_bx = nrm((L, 2, RG_WIDTH), 0.02)
    a0 = jax.random.uniform(next(ks), (L, 2, RG_WIDTH), f32, 0.9, 0.999)
    s0 = a0 ** (1.0 / RG_C)
    rg_lambda = jnp.log(s0) - jnp.log1p(-s0)
    w_gate = nrm((L, N_BRANCHES, D, D), D ** -0.5)
    b_gate = nrm((L, N_BRANCHES, D), 0.02)
    w_br = nrm((L, N_BRANCHES, BRANCH_WIDTH, D), BRANCH_WIDTH ** -0.5)
    w_out = nrm((L, D, D), D ** -0.5)
    w_up = nrm((L, D, 2 * D_FF), D ** -0.5)
    ffn_conv_w = nrm((L, FFN_CONV, 2 * D_FF), FFN_CONV ** -0.5)
    ffn_conv_b = nrm((L, 2 * D_FF), 0.02)
    w_down = nrm((L, D_FF, D), D_FF ** -0.5)
    return {'x': x, 'c': c, 'ctx': ctx, 'c_ctx': c_ctx,
            'w_ada': w_ada, 'b_ada': b_ada, 'norm1_g': norm1_g, 'norm2_g': norm2_g, 'w_in': w_in,
            'ssd_conv_w': ssd_conv_w, 'ssd_conv_b': ssd_conv_b, 'ssd_dt_bias': ssd_dt_bias,
            'ssd_a_log': ssd_a_log, 'ssd_d': ssd_d, 'ssd_norm_g': ssd_norm_g,
            'gqa_qnorm_g': gqa_qnorm_g, 'gqa_knorm_g': gqa_knorm_g,
            'diff_qnorm_g': diff_qnorm_g, 'diff_knorm_g': diff_knorm_g,
            'diff_lambda': diff_lambda, 'diff_subln_g': diff_subln_g,
            'rg_conv_w': rg_conv_w, 'rg_conv_b': rg_conv_b, 'rg_wa': rg_wa, 'rg_ba': rg_ba,
            'rg_wx': rg_wx, 'rg_bx': rg_bx, 'rg_lambda': rg_lambda,
            'w_gate': w_gate, 'b_gate': b_gate, 'w_br': w_br, 'w_out': w_out,
            'w_up': w_up, 'ffn_conv_w': ffn_conv_w, 'ffn_conv_b': ffn_conv_b, 'w_down': w_down}


def reference(x, c, ctx, c_ctx, w_ada, b_ada, norm1_g, norm2_g, w_in,
              ssd_conv_w, ssd_conv_b, ssd_dt_bias, ssd_a_log, ssd_d, ssd_norm_g,
              gqa_qnorm_g, gqa_knorm_g, diff_qnorm_g, diff_knorm_g, diff_lambda, diff_subln_g,
              rg_conv_w, rg_conv_b, rg_wa, rg_ba, rg_wx, rg_bx, rg_lambda,
              w_gate, b_gate, w_br, w_out, w_up, ffn_conv_w, ffn_conv_b, w_down):
    n_lat = x.shape[1]
    rows = n_lat // GRID_W
    row = jnp.repeat(jnp.arange(rows, dtype=jnp.float32), GRID_W)
    col = jnp.tile(jnp.arange(GRID_W, dtype=jnp.float32), rows)
    cos, sin = axial_rope(row, col)
    x_l, x_c = x, ctx
    for l in range(DEPTH):
        p = {'w_ada': w_ada[l], 'b_ada': b_ada[l], 'norm1_g': norm1_g[l], 'norm2_g': norm2_g[l],
             'w_in': w_in[l], 'ssd_conv_w': ssd_conv_w[l], 'ssd_conv_b': ssd_conv_b[l],
             'ssd_dt_bias': ssd_dt_bias[l], 'ssd_a_log': ssd_a_log[l], 'ssd_d': ssd_d[l],
             'ssd_norm_g': ssd_norm_g[l], 'gqa_qnorm_g': gqa_qnorm_g[l], 'gqa_knorm_g': gqa_knorm_g[l],
             'diff_qnorm_g': diff_qnorm_g[l], 'diff_knorm_g': diff_knorm_g[l],
             'diff_lambda': diff_lambda[l], 'diff_subln_g': diff_subln_g[l],
             'rg_conv_w': rg_conv_w[l], 'rg_conv_b': rg_conv_b[l], 'rg_wa': rg_wa[l], 'rg_ba': rg_ba[l],
             'rg_wx': rg_wx[l], 'rg_bx': rg_bx[l], 'rg_lambda': rg_lambda[l],
             'w_gate': w_gate[l], 'b_gate': b_gate[l], 'w_br': w_br[l], 'w_out': w_out[l],
             'w_up': w_up[l], 'ffn_conv_w': ffn_conv_w[l], 'ffn_conv_b': ffn_conv_b[l], 'w_down': w_down[l]}
        lambda_init = 0.8 - 0.6 * math.exp(-0.3 * l)
        x_l, x_c = layer(x_l, x_c, c, c_ctx, p, cos, sin, lambda_init, l < DEPTH - 1)
    return x_l
```

```python
import functools
import math

import jax
import jax.numpy as jnp
from jax import lax
from jax.experimental import pallas as pl
from jax.experimental.pallas import tpu as pltpu

F32 = jnp.float32
BF16 = jnp.bfloat16

GRID_W = 64
HEAD_DIM = 64
ROPE_BASE = 10000.0
NORM_EPS = 1e-6
SSD_HEADS = 8
SSD_HEAD_DIM = 64
SSD_INNER = 512
SSD_GROUPS = 2
SSD_STATE = 128
SSD_XBC = 1024
SSD_CHUNK = 128
GQA_WIDTH = 512
GQA_KV_WIDTH = 128
DIFF_HEADS = 4
DIFF_WIDTH = 512
RG_WIDTH = 512
RG_BLOCKS = 8
RG_C = 8.0
BRANCH_WIDTH = 512
N_BRANCHES = 4
IN_SPLITS = (512, 1024, 16, 512, 128, 128, 512, 512, 512, 512, 512)
DT_PAD = 128

VMEM_LIMIT_BYTES = 56 * 1024 * 1024
LANES = 128
ROW_HALO = 16
CHUNK = 128


def _cparams(n):
    return pltpu.CompilerParams(dimension_semantics=("parallel",) * n, vmem_limit_bytes=VMEM_LIMIT_BYTES)


def _resident(shape):
    nd = len(shape)
    return pl.BlockSpec(shape, lambda *_: (0,) * nd, pipeline_mode=pl.Buffered(1))


def _silu(x):
    return x * jax.nn.sigmoid(x)


def _dot(a, b):
    return jnp.dot(a, b, preferred_element_type=F32)


def _row_tile(n, want):
    t = min(want, n)
    while n % t:
        t //= 2
    return t


def _ada_kernel(c_ref, w_ref, b_ref, o_ref):
    h = _silu(c_ref[...]).astype(BF16)
    o_ref[...] = _dot(h, w_ref[...].astype(BF16)) + b_ref[...]


def _ada(cc, w, b):
    r, d = cc.shape
    n = w.shape[1]
    tn = _row_tile(n, 1536)
    return pl.pallas_call(
        _ada_kernel,
        grid=(n // tn,),
        in_specs=[pl.BlockSpec((r, d), lambda j: (0, 0)),
                  pl.BlockSpec((d, tn), lambda j: (0, j)),
                  pl.BlockSpec((1, tn), lambda j: (0, j))],
        out_specs=pl.BlockSpec((r, tn), lambda j: (0, j)),
        out_shape=jax.ShapeDtypeStruct((r, n), F32),
        compiler_params=_cparams(1),
        name="ada",
    )(cc, w, b.reshape(1, n))


def _proj_kernel(x_ref, sh_ref, sc_ref, g_ref, w_ref, b_ref, *out_refs, widths, acts):
    xf = x_ref[...]
    ms = jnp.mean(xf * xf, axis=-1, keepdims=True)
    y = xf * lax.rsqrt(ms + NORM_EPS) * g_ref[...]
    h = (y * (1.0 + sc_ref[0]) + sh_ref[0]).astype(BF16)
    c0 = 0
    for o_ref, w, act in zip(out_refs, widths, acts):
        for s in range(0, w, 512):
            e = min(s + 512, w)
            acc = _dot(h, w_ref[:, c0 + s:c0 + e])
            if act == "sigmoid":
                acc = jax.nn.sigmoid(acc + b_ref[:, c0 + s:c0 + e])
            o_ref[:, s:e] = acc.astype(o_ref.dtype)
        c0 += w


def _proj(x2d, shift, scale, gain, w_cat, b_cat, widths, acts, dtypes, rows_per_mod, tm):
    t, d = x2d.shape
    tm = _row_tile(min(t, rows_per_mod), tm)
    ncat = w_cat.shape[1]
    kern = functools.partial(_proj_kernel, widths=tuple(widths), acts=tuple(acts))
    mod_spec = pl.BlockSpec((1, 1, d), lambda i: ((i * tm) // rows_per_mod, 0, 0))
    return pl.pallas_call(
        kern,
        grid=(t // tm,),
        in_specs=[pl.BlockSpec((tm, d), lambda i: (i, 0)), mod_spec, mod_spec,
                  _resident((1, d)), _resident((d, ncat)), _resident((1, ncat))],
        out_specs=[pl.BlockSpec((tm, w), lambda i: (i, 0)) for w in widths],
        out_shape=[jax.ShapeDtypeStruct((t, w), dt) for w, dt in zip(widths, dtypes)],
        compiler_params=_cparams(1),
        name="proj",
    )(x2d, shift, scale, gain.reshape(1, d), w_cat, b_cat)


def _conv_window(src, n, r0, rows):
    cur = src[0, pl.ds(r0, rows), :].astype(F32)
    lo = pl.multiple_of(jnp.maximum(r0 - ROW_HALO, 0), ROW_HALO)
    hi = pl.multiple_of(jnp.minimum(r0 + rows, n - ROW_HALO), ROW_HALO)
    prev = src[0, pl.ds(lo, ROW_HALO), :].astype(F32)
    nxt = src[0, pl.ds(hi, ROW_HALO), :].astype(F32)
    prev = jnp.where(r0 > 0, prev, 0.0)
    nxt = jnp.where(r0 + rows < n, nxt, 0.0)
    return jnp.concatenate([prev, cur, nxt], axis=0)


def _dwconv4(win, rows, cw_ref, cb_ref):
    total = rows + 2 * ROW_HALO
    acc = cb_ref[...] + cw_ref[1:2, :] * win[ROW_HALO:ROW_HALO + rows]
    for k in (0, 2, 3):
        sh = pltpu.roll(win, (1 - k) % total, axis=0)[ROW_HALO:ROW_HALO + rows]
        acc = acc + cw_ref[k:k + 1, :] * sh
    return acc


def _cumsum_rows(a, reverse):
    n = a.shape[0]
    rid = lax.broadcasted_iota(jnp.int32, a.shape, 0)
    k = 1
    while k < n:
        if reverse:
            a = a + jnp.where(rid < n - k, pltpu.roll(a, n - k, axis=0), 0.0)
        else:
            a = a + jnp.where(rid >= k, pltpu.roll(a, k, axis=0), 0.0)
        k *= 2
    return a


def _per_head_cols(cols, width):
    rows = cols[0].shape[0]
    lane = lax.broadcasted_iota(jnp.int32, (rows, len(cols) * width), 1)
    out = jnp.broadcast_to(cols[-1], lane.shape)
    for j in range(len(cols) - 2, -1, -1):
        out = jnp.where(lane < (j + 1) * width, cols[j], out)
    return out


def _ssd_kernel(xbc_c, dt_c, z_c, xbc_l, dt_l, z_l, cw_ref, cb_ref, dtb_ref, aneg_ref, dvec_ref, ng_ref,
                out_c, out_l, xc_c, xc_l, yf_c, yf_l, st_ref, y_ref, *, m_ctx, n_lat):
    seqs = ((xbc_c, dt_c, z_c, out_c, xc_c, yf_c, m_ctx), (xbc_l, dt_l, z_l, out_l, xc_l, yf_l, n_lat))
    heads_per_group = SSD_HEADS // SSD_GROUPS
    gw = heads_per_group * SSD_HEAD_DIM

    for xbc, _, _, _, xc, _, n in seqs:
        def conv_body(j, carry, xbc=xbc, xc=xc, n=n):
            r0 = pl.multiple_of(j * CHUNK, CHUNK)
            win = _conv_window(xbc, n, r0, CHUNK)
            xc[pl.ds(r0, CHUNK), :] = _silu(_dwconv4(win, CHUNK, cw_ref, cb_ref)).astype(BF16)
            return carry
        lax.fori_loop(0, n // CHUNK, conv_body, 0)

    rid = lax.broadcasted_iota(jnp.int32, (CHUNK, CHUNK), 0)
    cid = lax.broadcasted_iota(jnp.int32, (CHUNK, CHUNK), 1)

    def chunk(xc, dtr, r0, d):
        dt = jax.nn.softplus(dtr[0, pl.ds(r0, CHUNK), :] + dtb_ref[...])
        a = dt * aneg_ref[...]
        cs = _cumsum_rows(a, reverse=(d == 1))
        cst = cs.T
        tot = cs[CHUNK - 1:CHUNK, :] if d == 0 else cs[0:1, :]
        mask = (rid >= cid) if d == 0 else (cid >= rid)
        ecs = jnp.exp(cs)
        edec = jnp.exp(tot - cs)
        etot = jnp.exp(tot)
        for g in range(SSD_GROUPS):
            bg = xc[pl.ds(r0, CHUNK), SSD_INNER + g * SSD_STATE:SSD_INNER + (g + 1) * SSD_STATE]
            cg = xc[pl.ds(r0, CHUNK), SSD_INNER + (SSD_GROUPS + g) * SSD_STATE:
                    SSD_INNER + (SSD_GROUPS + g + 1) * SSD_STATE]
            bgt = bg.astype(F32).T.astype(BF16)
            gmat = _dot(cg, bgt)
            xdec = []
            for hl in range(heads_per_group):
                h = g * heads_per_group + hl
                col = d * SSD_HEADS + h
                diff = cs[:, col:col + 1] - cst[col:col + 1, :]
                dec = jnp.exp(jnp.where(mask, diff, -jnp.inf))
                sc = (gmat * dec).astype(BF16)
                xh = xc[pl.ds(r0, CHUNK), h * SSD_HEAD_DIM:(h + 1) * SSD_HEAD_DIM].astype(F32)
                xdt = xh * dt[:, col:col + 1]
                y_ref[:, h * SSD_HEAD_DIM:(h + 1) * SSD_HEAD_DIM] = _dot(sc, xdt.astype(BF16))
                xdec.append((xdt * edec[:, col:col + 1]).astype(BF16))
            c0 = d * SSD_HEADS + g * heads_per_group
            st = st_ref[g]
            yoff = _dot(cg, st.astype(BF16))
            erow = _per_head_cols([ecs[:, c0 + j:c0 + j + 1] for j in range(heads_per_group)], SSD_HEAD_DIM)
            y_ref[:, g * gw:(g + 1) * gw] = y_ref[:, g * gw:(g + 1) * gw] + yoff * erow
            trow = _per_head_cols([etot[:, c0 + j:c0 + j + 1] for j in range(heads_per_group)], SSD_HEAD_DIM)
            xd = jnp.concatenate(xdec, axis=1)
            st_ref[g] = st * trow + _dot(bgt, xd)

    st_ref[...] = jnp.zeros_like(st_ref)
    for _, dtr, _, _, xc, yf, n in seqs:
        def fwd_body(j, carry, dtr=dtr, xc=xc, yf=yf):
            r0 = pl.multiple_of(j * CHUNK, CHUNK)
            chunk(xc, dtr, r0, 0)
            yf[pl.ds(r0, CHUNK), :] = y_ref[...]
            return carry
        lax.fori_loop(0, n // CHUNK, fwd_body, 0)

    st_ref[...] = jnp.zeros_like(st_ref)
    for _, dtr, z, out, xc, yf, n in seqs:
        def bwd_body(j, carry, dtr=dtr, z=z, out=out, xc=xc, yf=yf, n=n):
            r0 = pl.multiple_of((n // CHUNK - 1 - j) * CHUNK, CHUNK)
            chunk(xc, dtr, r0, 1)
            xs = xc[pl.ds(r0, CHUNK), 0:SSD_INNER].astype(F32)
            y = yf[pl.ds(r0, CHUNK), :] + y_ref[...] + xs * dvec_ref[...]
            y = y * _silu(z[0, pl.ds(r0, CHUNK), :].astype(F32))
            gsz = SSD_INNER // SSD_GROUPS
            for g in range(SSD_GROUPS):
                yg = y[:, g * gsz:(g + 1) * gsz]
                ms = jnp.mean(yg * yg, axis=-1, keepdims=True)
                yn = yg * lax.rsqrt(ms + NORM_EPS) * ng_ref[:, g * gsz:(g + 1) * gsz]
                out[0, pl.ds(r0, CHUNK), g * gsz:(g + 1) * gsz] = yn.astype(out.dtype)
            return carry
        lax.fori_loop(0, n // CHUNK, bwd_body, 0)


def _ssd(xbc_c, dt_c, z_c, xbc_l, dt_l, z_l, conv_w, conv_b, dt_bias, a_log, d_skip, norm_g, bsz):
    m_ctx = xbc_c.shape[0] // bsz
    n_lat = xbc_l.shape[0] // bsz
    r3 = lambda a, n: a.reshape(bsz, n, a.shape[-1])
    pad = DT_PAD - 2 * SSD_HEADS
    dtb = jnp.pad(dt_bias.astype(F32).reshape(1, -1), ((0, 0), (0, pad)))
    aneg = jnp.pad(-jnp.exp(a_log.astype(F32)).reshape(1, -1), ((0, 0), (0, pad)))
    dvec = jnp.repeat(d_skip.astype(F32), SSD_HEAD_DIM).reshape(1, SSD_INNER)

    def seq_spec(n, w, single):
        kw = dict(pipeline_mode=pl.Buffered(1)) if single else {}
        return pl.BlockSpec((1, n, w), lambda b: (b, 0, 0), **kw)

    kern = functools.partial(_ssd_kernel, m_ctx=m_ctx, n_lat=n_lat)
    out_c, out_l = pl.pallas_call(
        kern,
        grid=(bsz,),
        in_specs=[seq_spec(m_ctx, SSD_XBC, False), seq_spec(m_ctx, DT_PAD, False), seq_spec(m_ctx, SSD_INNER, False),
                  seq_spec(n_lat, SSD_XBC, True), seq_spec(n_lat, DT_PAD, False), seq_spec(n_lat, SSD_INNER, True),
                  _resident((4, SSD_XBC)), _resident((1, SSD_XBC)), _resident((1, DT_PAD)), _resident((1, DT_PAD)),
                  _resident((1, SSD_INNER)), _resident((1, SSD_INNER))],
        out_specs=[seq_spec(m_ctx, SSD_INNER, False), seq_spec(n_lat, SSD_INNER, False)],
        out_shape=[jax.ShapeDtypeStruct((bsz, m_ctx, SSD_INNER), BF16),
                   jax.ShapeDtypeStruct((bsz, n_lat, SSD_INNER), BF16)],
        scratch_shapes=[pltpu.VMEM((m_ctx, SSD_XBC), BF16), pltpu.VMEM((n_lat, SSD_XBC), BF16),
                        pltpu.VMEM((m_ctx, SSD_INNER), F32), pltpu.VMEM((n_lat, SSD_INNER), F32),
                        pltpu.VMEM((SSD_GROUPS, SSD_STATE, SSD_INNER // SSD_GROUPS), F32),
                        pltpu.VMEM((CHUNK, SSD_INNER), F32)],
        compiler_params=_cparams(1),
        name="ssd",
    )(r3(xbc_c, m_ctx), r3(dt_c, m_ctx), r3(z_c, m_ctx), r3(xbc_l, n_lat), r3(dt_l, n_lat), r3(z_l, n_lat),
      conv_w.astype(F32), conv_b.astype(F32).reshape(1, -1), dtb, aneg, dvec, norm_g.astype(F32).reshape(1, -1))
    return out_c.reshape(bsz * m_ctx, SSD_INNER), out_l.reshape(bsz * n_lat, SSD_INNER)


def _gelu_tanh(x):
    return 0.5 * x * (1.0 + jnp.tanh(math.sqrt(2.0 / math.pi) * (x + 0.044715 * (x * x * x))))


def _rg_kernel(x_c, g_c, x_l, g_l, cw_ref, cb_ref, w_ref, b_ref, lam_ref,
               out_c, out_l, hf_c, hf_l, a_s, b_s, h_s, hcar, *, m_ctx, n_lat):
    seqs = ((x_c, g_c, out_c, hf_c, m_ctx), (x_l, g_l, out_l, hf_l, n_lat))
    gw = 2 * RG_WIDTH

    def chunk(x, n, r0, d):
        win = _conv_window(x, n, r0, CHUNK)
        xr = _dwconv4(win, CHUNK, cw_ref, cb_ref)
        gates = _dot(xr.astype(BF16), w_ref[:, d * gw:(d + 1) * gw]) + b_ref[:, d * gw:(d + 1) * gw]
        sp = jax.nn.softplus(-lam_ref[d:d + 1, :])
        log_a = -RG_C * jax.nn.sigmoid(gates[:, :RG_WIDTH]) * sp
        a = jnp.exp(log_a)
        a_s[...] = a
        one_minus_a2 = -jnp.tanh(log_a) * (a * a + 1.0)
        b_s[...] = jnp.sqrt(one_minus_a2) * (jax.nn.sigmoid(gates[:, RG_WIDTH:]) * xr)

        def step(i, h):
            t = i if d == 0 else CHUNK - 1 - i
            h = a_s[pl.ds(t, 1), :] * h + b_s[pl.ds(t, 1), :]
            h_s[pl.ds(t, 1), :] = h
            return h
        hcar[d:d + 1, :] = lax.fori_loop(0, CHUNK, step, hcar[d:d + 1, :], unroll=8)

    hcar[...] = jnp.zeros_like(hcar)
    for x, _, _, hf, n in seqs:
        def fwd_body(j, carry, x=x, hf=hf, n=n):
            r0 = pl.multiple_of(j * CHUNK, CHUNK)
            chunk(x, n, r0, 0)
            hf[pl.ds(r0, CHUNK), :] = h_s[...]
            return carry
        lax.fori_loop(0, n // CHUNK, fwd_body, 0)

    for x, gg, out, hf, n in seqs:
        def bwd_body(j, carry, x=x, gg=gg, out=out, hf=hf, n=n):
            r0 = pl.multiple_of((n // CHUNK - 1 - j) * CHUNK, CHUNK)
            chunk(x, n, r0, 1)
            hsum = hf[pl.ds(r0, CHUNK), :] + h_s[...]
            gate = _gelu_tanh(gg[0, pl.ds(r0, CHUNK), :].astype(F32))
            out[0, pl.ds(r0, CHUNK), :] = (hsum * gate).astype(out.dtype)
            return carry
        lax.fori_loop(0, n // CHUNK, bwd_body, 0)


def _block_diag(w):
    k, c, e = w.shape
    return jnp.einsum("kce,kj->kcje", w, jnp.eye(k, dtype=w.dtype)).reshape(k * c, k * e)


def _rglru(rgx_c, rgg_c, rgx_l, rgg_l, conv_w, conv_b, wa, ba, wx, bx, lam, bsz):
    m_ctx = rgx_c.shape[0] // bsz
    n_lat = rgx_l.shape[0] // bsz
    r3 = lambda a, n: a.reshape(bsz, n, a.shape[-1])
    w_cat = jnp.concatenate([_block_diag(wa[0]), _block_diag(wx[0]), _block_diag(wa[1]), _block_diag(wx[1])],
                            axis=1).astype(BF16)
    b_cat = jnp.concatenate([ba[0], bx[0], ba[1], bx[1]]).astype(F32).reshape(1, -1)

    def seq_spec(n):
        return pl.BlockSpec((1, n, RG_WIDTH), lambda b: (b, 0, 0))

    kern = functools.partial(_rg_kernel, m_ctx=m_ctx, n_lat=n_lat)
    out_c, out_l = pl.pallas_call(
        kern,
        grid=(bsz,),
        in_specs=[seq_spec(m_ctx), seq_spec(m_ctx), seq_spec(n_lat), seq_spec(n_lat),
                  _resident((4, RG_WIDTH)), _resident((1, RG_WIDTH)), _resident((RG_WIDTH, 4 * RG_WIDTH)),
                  _resident((1, 4 * RG_WIDTH)), _resident((2, RG_WIDTH))],
        out_specs=[seq_spec(m_ctx), seq_spec(n_lat)],
        out_shape=[jax.ShapeDtypeStruct((bsz, m_ctx, RG_WIDTH), BF16),
                   jax.ShapeDtypeStruct((bsz, n_lat, RG_WIDTH), BF16)],
        scratch_shapes=[pltpu.VMEM((m_ctx, RG_WIDTH), F32), pltpu.VMEM((n_lat, RG_WIDTH), F32),
                        pltpu.VMEM((CHUNK, RG_WIDTH), F32), pltpu.VMEM((CHUNK, RG_WIDTH), F32),
                        pltpu.VMEM((CHUNK, RG_WIDTH), F32), pltpu.VMEM((2, RG_WIDTH), F32)],
        compiler_params=_cparams(1),
        name="rglru",
    )(r3(rgx_c, m_ctx), r3(rgg_c, m_ctx), r3(rgx_l, n_lat), r3(rgg_l, n_lat),
      conv_w.astype(F32), conv_b.astype(F32).reshape(1, -1), w_cat, b_cat, lam.astype(F32))
    return out_c.reshape(bsz * m_ctx, RG_WIDTH), out_l.reshape(bsz * n_lat, RG_WIDTH)


def _prep_kernel(x_ref, g_ref, *rest, width, rope, transpose, scale):
    if rope:
        cos_ref, sin_ref, o_ref = rest
    else:
        (o_ref,) = rest
    x = x_ref[...].astype(F32)
    sq = x * x
    hi = sq.astype(BF16)
    lo = (sq - hi.astype(F32)).astype(BF16)
    seg_r = lax.broadcasted_iota(jnp.int32, (width, width), 0) // HEAD_DIM
    seg_c = lax.broadcasted_iota(jnp.int32, (width, width), 1) // HEAD_DIM
    ones_bd = jnp.where(seg_r == seg_c, 1.0, 0.0).astype(BF16)
    ss = _dot(hi, ones_bd) + _dot(lo, ones_bd)
    y = x * lax.rsqrt(ss * (1.0 / HEAD_DIM) + NORM_EPS) * g_ref[...]
    if rope:
        cos = cos_ref[...]
        sin = sin_ref[...]
        lane = lax.broadcasted_iota(jnp.int32, (x.shape[0], LANES), 1)
        first = (lane % HEAD_DIM) < (HEAD_DIM // 2)
        cols = []
        for j in range(width // LANES):
            yj = y[:, j * LANES:(j + 1) * LANES]
            partner = jnp.where(first, pltpu.roll(yj, LANES - HEAD_DIM // 2, axis=1),
                                pltpu.roll(yj, HEAD_DIM // 2, axis=1))
            cols.append(yj * cos + partner * sin)
        y = cols[0] if len(cols) == 1 else jnp.concatenate(cols, axis=1)
    if scale != 1.0:
        y = y * scale
    if transpose:
        o_ref[0] = y.T.astype(o_ref.dtype)
    else:
        o_ref[...] = y.astype(o_ref.dtype)


def _prep(x2d, gain, cos, sin, bsz, transpose, scale):
    t, width = x2d.shape
    ntok = t // bsz
    tt = _row_tile(ntok, 512)
    nt = ntok // tt
    rope = cos is not None
    g_t = jnp.tile(gain.astype(F32), width // HEAD_DIM).reshape(1, width)
    in_specs = [pl.BlockSpec((tt, width), lambda b, i: (b * nt + i, 0)), _resident((1, width))]
    args = [x2d, g_t]
    if rope:
        in_specs += [pl.BlockSpec((tt, LANES), lambda b, i: (i, 0))] * 2
        args += [cos, sin]
    if transpose:
        out_spec = pl.BlockSpec((1, width, tt), lambda b, i: (b, 0, i))
        out_shape = jax.ShapeDtypeStruct((bsz, width, ntok), BF16)
    else:
        out_spec = pl.BlockSpec((tt, width), lambda b, i: (b * nt + i, 0))
        out_shape = jax.ShapeDtypeStruct((t, width), BF16)
    kern = functools.partial(_prep_kernel, width=width, rope=rope, transpose=transpose, scale=scale)
    return pl.pallas_call(kern, grid=(bsz, nt), in_specs=in_specs, out_specs=out_spec, out_shape=out_shape,
                          compiler_params=_cparams(2), name="qk_prep")(*args)


KEY_CHUNK = 512


def _over_keys(step, carry, kt_ref, v_ref, k_rows):
    ns = kt_ref.shape[2]
    if ns <= KEY_CHUNK:
        return step(carry, kt_ref[0, k_rows, :], v_ref[0])
    tk = KEY_CHUNK
    while ns % tk:
        tk //= 2

    def body(j, c):
        o = pl.multiple_of(j * tk, tk)
        return step(c, kt_ref[0, k_rows, pl.ds(o, tk)], v_ref[0, pl.ds(o, tk), :])
    return lax.fori_loop(0, ns // tk, body, carry)


def _softmax_step(q, carry, kt, v):
    m, l, acc = carry
    s = _dot(q, kt)
    m_new = jnp.maximum(m, jnp.max(s, axis=-1, keepdims=True))
    alpha = jnp.exp(m - m_new)
    p = jnp.exp(s - m_new)
    l = alpha * l + jnp.sum(p, axis=-1, keepdims=True)
    acc = alpha * acc + _dot(p.astype(BF16), v)
    return m_new, l, acc


def _softmax_init(rows, cols):
    return (jnp.full((rows, 1), -jnp.inf, F32), jnp.zeros((rows, 1), F32), jnp.zeros((rows, cols), F32))


def _gqa_kernel(q_ref, *rest, n_seg, tq, rep):
    o_ref = rest[-1]
    q = q_ref[...]
    qs = jnp.concatenate([q[:, r * HEAD_DIM:(r + 1) * HEAD_DIM] for r in range(rep)], axis=0)
    carry = _softmax_init(rep * tq, GQA_KV_WIDTH)
    step = functools.partial(_softmax_step, qs)
    for s in range(n_seg):
        carry = _over_keys(step, carry, rest[2 * s], rest[2 * s + 1], slice(None))
    _, l, acc = carry
    o = acc / l
    g = pl.program_id(1)
    osel = jnp.where(g == 0, o[:, :HEAD_DIM], o[:, HEAD_DIM:])
    for r in range(rep):
        o_ref[:, r * HEAD_DIM:(r + 1) * HEAD_DIM] = osel[r * tq:(r + 1) * tq].astype(o_ref.dtype)


def _gqa_attention(q2d, segs, bsz):
    t = q2d.shape[0]
    nq = t // bsz
    kv_heads = GQA_KV_WIDTH // HEAD_DIM
    rep = GQA_WIDTH // GQA_KV_WIDTH
    tq = _row_tile(nq, 128)
    nt = nq // tq
    qw = rep * HEAD_DIM
    in_specs = [pl.BlockSpec((tq, qw), lambda b, g, i: (b * nt + i, g))]
    args = [q2d]
    for kt, v in segs:
        ns = kt.shape[2]
        in_specs += [pl.BlockSpec((1, HEAD_DIM, ns), lambda b, g, i: (b, g, 0)),
                     pl.BlockSpec((1, ns, GQA_KV_WIDTH), lambda b, g, i: (b, 0, 0))]
        args += [kt, v]
    kern = functools.partial(_gqa_kernel, n_seg=len(segs), tq=tq, rep=rep)
    return pl.pallas_call(
        kern, grid=(bsz, kv_heads, nt), in_specs=in_specs,
        out_specs=pl.BlockSpec((tq, qw), lambda b, g, i: (b * nt + i, g)),
        out_shape=jax.ShapeDtypeStruct((t, GQA_WIDTH), BF16),
        compiler_params=_cparams(3), name="gqa_attn")(*args)


def _diff_kernel(q_ref, lam_ref, sg_ref, *rest, n_seg, lambda_init):
    o_ref = rest[-1]
    q = q_ref[...]
    tq = q.shape[0]
    vw = 2 * HEAD_DIM
    carries = []
    for c in range(2):
        qc = q[:, c * HEAD_DIM:(c + 1) * HEAD_DIM]
        carry = _softmax_init(tq, vw)
        step = functools.partial(_softmax_step, qc)
        for s in range(n_seg):
            carry = _over_keys(step, carry, rest[2 * s], rest[2 * s + 1], slice(c * HEAD_DIM, (c + 1) * HEAD_DIM))
        carries.append(carry)
    lp = lam_ref[...]
    lam = (jnp.exp(jnp.sum(lp[0:1] * lp[1:2], axis=-1, keepdims=True))
           - jnp.exp(jnp.sum(lp[2:3] * lp[3:4], axis=-1, keepdims=True)) + lambda_init)
    o = carries[0][2] / carries[0][1] - lam * (carries[1][2] / carries[1][1])
    ms = jnp.mean(o * o, axis=-1, keepdims=True)
    o = o * lax.rsqrt(ms + NORM_EPS) * sg_ref[...] * (1.0 - lambda_init)
    o_ref[...] = o.astype(o_ref.dtype)


def _diff_attention(q2d, segs, lam_p, subln_g, lambda_init, bsz):
    t = q2d.shape[0]
    nq = t // bsz
    tq = _row_tile(nq, 512)
    nt = nq // tq
    vw = 2 * HEAD_DIM
    in_specs = [pl.BlockSpec((tq, vw), lambda b, h, i: (b * nt + i, h)),
                _resident((4, HEAD_DIM)), _resident((1, vw))]
    args = [q2d, lam_p.astype(F32), subln_g.astype(F32).reshape(1, vw)]
    for kt, v in segs:
        ns = kt.shape[2]
        in_specs += [pl.BlockSpec((1, vw, ns), lambda b, h, i: (b, h, 0)),
                     pl.BlockSpec((1, ns, vw), lambda b, h, i: (b, 0, h))]
        args += [kt, v]
    kern = functools.partial(_diff_kernel, n_seg=len(segs), lambda_init=lambda_init)
    return pl.pallas_call(
        kern, grid=(bsz, DIFF_HEADS, nt), in_specs=in_specs,
        out_specs=pl.BlockSpec((tq, vw), lambda b, h, i: (b * nt + i, h)),
        out_shape=jax.ShapeDtypeStruct((t, DIFF_WIDTH), BF16),
        compiler_params=_cparams(3), name="diff_attn")(*args)


def _merge_kernel(x_ref, g1_ref, gates_ref, o0, o1, o2, o3, wbr_ref, wout_ref, out_ref):
    d = x_ref.shape[1]
    m = None
    for k, o in enumerate((o0, o1, o2, o3)):
        t = gates_ref[:, k * d:(k + 1) * d].astype(F32) * _dot(o[...], wbr_ref[k])
        m = t if m is None else m + t
    y = _dot(m.astype(BF16), wout_ref[...])
    out_ref[...] = x_ref[...] + g1_ref[0] * y


def _merge(x2d, g1, gates, outs, w_br, w_out, rows_per_mod):
    t, d = x2d.shape
    tm = _row_tile(min(t, rows_per_mod), 512)
    row = lambda w: pl.BlockSpec((tm, w), lambda i: (i, 0))
    return pl.pallas_call(
        _merge_kernel,
        grid=(t // tm,),
        in_specs=[row(d), pl.BlockSpec((1, 1, d), lambda i: ((i * tm) // rows_per_mod, 0, 0)), row(N_BRANCHES * d),
                  row(BRANCH_WIDTH), row(BRANCH_WIDTH), row(BRANCH_WIDTH), row(BRANCH_WIDTH),
                  _resident((N_BRANCHES, BRANCH_WIDTH, d)), _resident((d, d))],
        out_specs=row(d),
        out_shape=jax.ShapeDtypeStruct((t, d), F32),
        compiler_params=_cparams(1),
        name="merge",
    )(x2d, g1, gates, *outs, w_br, w_out)


FFN_COL_CHUNK = 256


def _ffn_kernel(x_ref, g2_ref, u_ref, up_ref, un_ref, cw_ref, cb_ref, wd_ref, out_ref, *, tiles_per_seq, d_ff):
    tm = u_ref.shape[0]
    i = pl.program_id(0)
    has_prev = (i % tiles_per_seq) != 0
    has_next = (i % tiles_per_seq) != tiles_per_seq - 1
    rid = lax.broadcasted_iota(jnp.int32, (tm, FFN_COL_CHUNK), 0)

    def conv(c0):
        cur = u_ref[:, c0:c0 + FFN_COL_CHUNK].astype(F32)
        prev_row = jnp.where(has_prev, up_ref[ROW_HALO - 1:ROW_HALO, c0:c0 + FFN_COL_CHUNK].astype(F32), 0.0)
        next_row = jnp.where(has_next, un_ref[0:1, c0:c0 + FFN_COL_CHUNK].astype(F32), 0.0)
        dn = jnp.where(rid == 0, prev_row, pltpu.roll(cur, 1, axis=0))
        up = jnp.where(rid == tm - 1, next_row, pltpu.roll(cur, tm - 1, axis=0))
        w = cw_ref[:, c0:c0 + FFN_COL_CHUNK]
        return w[0:1] * dn + w[1:2] * cur + w[2:3] * up + cb_ref[:, c0:c0 + FFN_COL_CHUNK]

    acc = jnp.zeros((tm, out_ref.shape[1]), F32)
    for c in range(d_ff // FFN_COL_CHUNK):
        c0 = c * FFN_COL_CHUNK
        act = (_silu(conv(c0)) * conv(d_ff + c0)).astype(BF16)
        acc = acc + _dot(act, wd_ref[c0:c0 + FFN_COL_CHUNK, :])
    out_ref[...] = x_ref[...] + g2_ref[0] * acc


def _ffn_down(x2d, g2, u, conv_w, conv_b, w_down, rows_per_mod, seq_len):
    t, d = x2d.shape
    d_ff = w_down.shape[0]
    tm = _row_tile(seq_len, 512)
    hb = tm // ROW_HALO
    n_halo = t // ROW_HALO
    kern = functools.partial(_ffn_kernel, tiles_per_seq=seq_len // tm, d_ff=d_ff)
    return pl.pallas_call(
        kern,
        grid=(t // tm,),
        in_specs=[pl.BlockSpec((tm, d), lambda i: (i, 0)),
                  pl.BlockSpec((1, 1, d), lambda i: ((i * tm) // rows_per_mod, 0, 0)),
                  pl.BlockSpec((tm, 2 * d_ff), lambda i: (i, 0)),
                  pl.BlockSpec((ROW_HALO, 2 * d_ff), lambda i: (jnp.maximum(i * hb - 1, 0), 0)),
                  pl.BlockSpec((ROW_HALO, 2 * d_ff), lambda i: (jnp.minimum((i + 1) * hb, n_halo - 1), 0)),
                  _resident((3, 2 * d_ff)), _resident((1, 2 * d_ff)), _resident((d_ff, d))],
        out_specs=pl.BlockSpec((tm, d), lambda i: (i, 0)),
        out_shape=jax.ShapeDtypeStruct((t, d), F32),
        compiler_params=_cparams(1),
        name="ffn_down",
    )(x2d, g2, u, u, u, conv_w.astype(F32), conv_b.astype(F32).reshape(1, -1), w_down)


def _rope_tables(n_lat):
    t = jnp.arange(n_lat)
    row = (t // GRID_W).astype(F32)
    col = (t % GRID_W).astype(F32)
    n_freq = HEAD_DIM // 4
    inv = ROPE_BASE ** (-jnp.arange(n_freq, dtype=F32) / n_freq)
    ang = jnp.concatenate([row[:, None] * inv, col[:, None] * inv], axis=-1)
    cos, sin = jnp.cos(ang), jnp.sin(ang)
    reps = LANES // HEAD_DIM
    return (jnp.tile(jnp.concatenate([cos, cos], axis=-1), (1, reps)),
            jnp.tile(jnp.concatenate([-sin, sin], axis=-1), (1, reps)))


IN_WIDTHS = (512, 1024, DT_PAD, 512, 128, 128, 512, 512, 512, 512, 512)


def kernel(x, c, ctx, c_ctx, w_ada, b_ada, norm1_g, norm2_g, w_in, ssd_conv_w, ssd_conv_b, ssd_dt_bias, ssd_a_log, ssd_d, ssd_norm_g, gqa_qnorm_g, gqa_knorm_g, diff_qnorm_g, diff_knorm_g, diff_lambda, diff_subln_g, rg_conv_w, rg_conv_b, rg_wa, rg_ba, rg_wx, rg_bx, rg_lambda, w_gate, b_gate, w_br, w_out, w_up, ffn_conv_w, ffn_conv_b, w_down):
    bsz, n_lat, d = x.shape
    m_ctx = ctx.shape[1]
    depth = w_in.shape[0]
    d_ff = w_down.shape[1]
    cos, sin = _rope_tables(n_lat)
    scale = HEAD_DIM ** -0.5

    x_l = x.reshape(bsz * n_lat, d)
    x_c = ctx.reshape(bsz * m_ctx, d)
    n_c = bsz + 1
    n_c_pad = -(-n_c // 8) * 8
    cc = jnp.pad(jnp.concatenate([c, c_ctx[None, :]], axis=0), ((0, n_c_pad - n_c), (0, 0)))
    dt_col = IN_SPLITS[0] + IN_SPLITS[1]

    for l in range(depth):
        with_ctx_out = l < depth - 1
        lambda_init = 0.8 - 0.6 * math.exp(-0.3 * l)

        mod = _ada(cc, w_ada[l], b_ada[l])
        mods_l = [mod[:bsz, k * d:(k + 1) * d].reshape(bsz, 1, d) for k in range(6)]
        mods_c = [mod[bsz:bsz + 1, k * d:(k + 1) * d].reshape(1, 1, d) for k in range(6)]

        wi = w_in[l]
        w_cat = jnp.concatenate(
            [wi[:, :dt_col + 16], jnp.zeros((d, DT_PAD - 16), wi.dtype), wi[:, dt_col + 16:],
             jnp.transpose(w_gate[l], (1, 0, 2)).reshape(d, N_BRANCHES * d)], axis=1).astype(BF16)
        b_cat = jnp.concatenate([jnp.zeros((sum(IN_WIDTHS),), F32), b_gate[l].reshape(-1).astype(F32)]).reshape(1, -1)
        widths = IN_WIDTHS + (N_BRANCHES * d,)
        acts = (None,) * len(IN_WIDTHS) + ("sigmoid",)
        dtypes = (BF16, BF16, F32) + (BF16,) * 9

        def in_proj(x2d, mods, rows_per_mod):
            return _proj(x2d, mods[0], mods[1], norm1_g[l], w_cat, b_cat, widths, acts, dtypes, rows_per_mod, 256)

        z_l, xbc_l, dt_l, gq_l, gk_l, gv_l, dq_l, dk_l, dv_l, rgg_l, rgx_l, gates_l = in_proj(x_l, mods_l, n_lat)
        z_c, xbc_c, dt_c, gq_c, gk_c, gv_c, dq_c, dk_c, dv_c, rgg_c, rgx_c, gates_c = in_proj(x_c, mods_c, bsz * m_ctx)

        ssd_c, ssd_l = _ssd(xbc_c, dt_c, z_c, xbc_l, dt_l, z_l, ssd_conv_w[l], ssd_conv_b[l], ssd_dt_bias[l],
                            ssd_a_log[l], ssd_d[l], ssd_norm_g[l], bsz)
        rg_c, rg_l = _rglru(rgx_c, rgg_c, rgx_l, rgg_l, rg_conv_w[l], rg_conv_b[l], rg_wa[l], rg_ba[l],
                            rg_wx[l], rg_bx[l], rg_lambda[l], bsz)

        r3 = lambda a, n: a.reshape(bsz, n, a.shape[-1])
        gkt_c = _prep(gk_c, gqa_knorm_g[l], None, None, bsz, True, 1.0)
        gkt_l = _prep(gk_l, gqa_knorm_g[l], cos, sin, bsz, True, 1.0)
        gqn_l = _prep(gq_l, gqa_qnorm_g[l], cos, sin, bsz, False, scale)
        gseg = [(gkt_c, r3(gv_c, m_ctx)), (gkt_l, r3(gv_l, n_lat))]
        gqa_l = _gqa_attention(gqn_l, gseg, bsz)
        dkt_c = _prep(dk_c, diff_knorm_g[l], None, None, bsz, True, 1.0)
        dkt_l = _prep(dk_l, diff_knorm_g[l], cos, sin, bsz, True, 1.0)
        dqn_l = _prep(dq_l, diff_qnorm_g[l], cos, sin, bsz, False, scale)
        dseg = [(dkt_c, r3(dv_c, m_ctx)), (dkt_l, r3(dv_l, n_lat))]
        diff_l = _diff_attention(dqn_l, dseg, diff_lambda[l], diff_subln_g[l], lambda_init, bsz)

        wbr = w_br[l].astype(BF16)
        wout = w_out[l].astype(BF16)
        wup = w_up[l].astype(BF16)
        wdn = w_down[l].astype(BF16)
        b_up = jnp.zeros((1, 2 * d_ff), F32)

        def ffn(x2d, mods, rows_per_mod, seq_len):
            (u,) = _proj(x2d, mods[3], mods[4], norm2_g[l], wup, b_up, (2 * d_ff,), (None,), (BF16,),
                         rows_per_mod, 256)
            return _ffn_down(x2d, mods[5], u, ffn_conv_w[l], ffn_conv_b[l], wdn, rows_per_mod, seq_len)

        x_l = _merge(x_l, mods_l[2], gates_l, (ssd_l, gqa_l, diff_l, rg_l), wbr, wout, n_lat)
        x_l = ffn(x_l, mods_l, n_lat, n_lat)
        if with_ctx_out:
            gqn_c = _prep(gq_c, gqa_qnorm_g[l], None, None, bsz, False, scale)
            gqa_c = _gqa_attention(gqn_c, gseg[:1], bsz)
            dqn_c = _prep(dq_c, diff_qnorm_g[l], None, None, bsz, False, scale)
            diff_c = _diff_attention(dqn_c, dseg[:1], diff_lambda[l], diff_subln_g[l], lambda_init, bsz)
            x_c = _merge(x_c, mods_c[2], gates_c, (ssd_c, gqa_c, diff_c, rg_c), wbr, wout, bsz * m_ctx)
            x_c = ffn(x_c, mods_c, bsz * m_ctx, m_ctx)
    return x_l.reshape(bsz, n_lat, d)
```

```python
import functools
import math

import jax
import jax.numpy as jnp
from jax import lax
from jax.experimental import pallas as pl
from jax.experimental.pallas import tpu as pltpu

F32 = jnp.float32
BF16 = jnp.bfloat16

GRID_W = 64
HEAD_DIM = 64
ROPE_BASE = 10000.0
NORM_EPS = 1e-6
SSD_HEADS = 8
SSD_HEAD_DIM = 64
SSD_INNER = 512
SSD_GROUPS = 2
SSD_STATE = 128
SSD_XBC = 1024
SSD_CHUNK = 128
GQA_WIDTH = 512
GQA_KV_WIDTH = 128
DIFF_HEADS = 4
DIFF_WIDTH = 512
RG_WIDTH = 512
RG_BLOCKS = 8
RG_C = 8.0
BRANCH_WIDTH = 512
N_BRANCHES = 4
IN_SPLITS = (512, 1024, 16, 512, 128, 128, 512, 512, 512, 512, 512)
DT_PAD = 128

VMEM_LIMIT_BYTES = 56 * 1024 * 1024
LANES = 128
ROW_HALO = 16
CHUNK = 128


def _cparams(n):
    return pltpu.CompilerParams(dimension_semantics=("parallel",) * n, vmem_limit_bytes=VMEM_LIMIT_BYTES)


def _resident(shape):
    nd = len(shape)
    return pl.BlockSpec(shape, lambda *_: (0,) * nd, pipeline_mode=pl.Buffered(1))


def _silu(x):
    return x * jax.nn.sigmoid(x)


def _dot(a, b):
    return jnp.dot(a, b, preferred_element_type=F32)


def _row_tile(n, want):
    t = min(want, n)
    while n % t:
        t //= 2
    return t


def _ada_kernel(c_ref, w_ref, b_ref, o_ref):
    h = _silu(c_ref[...]).astype(BF16)
    o_ref[...] = _dot(h, w_ref[...].astype(BF16)) + b_ref[...]


def _ada(cc, w, b):
    r, d = cc.shape
    n = w.shape[1]
    tn = _row_tile(n, 1536)
    return pl.pallas_call(
        _ada_kernel,
        grid=(n // tn,),
        in_specs=[pl.BlockSpec((r, d), lambda j: (0, 0)),
                  pl.BlockSpec((d, tn), lambda j: (0, j)),
                  pl.BlockSpec((1, tn), lambda j: (0, j))],
        out_specs=pl.BlockSpec((r, tn), lambda j: (0, j)),
        out_shape=jax.ShapeDtypeStruct((r, n), F32),
        compiler_params=_cparams(1),
        name="ada",
    )(cc, w, b.reshape(1, n))


def _proj_kernel(x_ref, sh_ref, sc_ref, g_ref, w_ref, b_ref, *out_refs, widths, acts):
    xf = x_ref[...]
    ms = jnp.mean(xf * xf, axis=-1, keepdims=True)
    y = xf * lax.rsqrt(ms + NORM_EPS) * g_ref[...]
    h = (y * (1.0 + sc_ref[0]) + sh_ref[0]).astype(BF16)
    c0 = 0
    for o_ref, w, act in zip(out_refs, widths, acts):
        for s in range(0, w, 512):
            e = min(s + 512, w)
            acc = _dot(h, w_ref[:, c0 + s:c0 + e])
            if act == "sigmoid":
                acc = jax.nn.sigmoid(acc + b_ref[:, c0 + s:c0 + e])
            o_ref[:, s:e] = acc.astype(o_ref.dtype)
        c0 += w


def _proj(x2d, shift, scale, gain, w_cat, b_cat, widths, acts, dtypes, rows_per_mod, tm):
    t, d = x2d.shape
    tm = _row_tile(min(t, rows_per_mod), tm)
    ncat = w_cat.shape[1]
    kern = functools.partial(_proj_kernel, widths=tuple(widths), acts=tuple(acts))
    mod_spec = pl.BlockSpec((1, 1, d), lambda i: ((i * tm) // rows_per_mod, 0, 0))
    return pl.pallas_call(
        kern,
        grid=(t // tm,),
        in_specs=[pl.BlockSpec((tm, d), lambda i: (i, 0)), mod_spec, mod_spec,
                  _resident((1, d)), _resident((d, ncat)), _resident((1, ncat))],
        out_specs=[pl.BlockSpec((tm, w), lambda i: (i, 0)) for w in widths],
        out_shape=[jax.ShapeDtypeStruct((t, w), dt) for w, dt in zip(widths, dtypes)],
        compiler_params=_cparams(1),
        name="proj",
    )(x2d, shift, scale, gain.reshape(1, d), w_cat, b_cat)


def _conv_window(src, n, r0, rows):
    cur = src[0, pl.ds(r0, rows), :].astype(F32)
    lo = pl.multiple_of(jnp.maximum(r0 - ROW_HALO, 0), ROW_HALO)
    hi = pl.multiple_of(jnp.minimum(r0 + rows, n - ROW_HALO), ROW_HALO)
    prev = src[0, pl.ds(lo, ROW_HALO), :].astype(F32)
    nxt = src[0, pl.ds(hi, ROW_HALO), :].astype(F32)
    prev = jnp.where(r0 > 0, prev, 0.0)
    nxt = jnp.where(r0 + rows < n, nxt, 0.0)
    return jnp.concatenate([prev, cur, nxt], axis=0)


def _dwconv4(win, rows, cw_ref, cb_ref):
    total = rows + 2 * ROW_HALO
    acc = cb_ref[...] + cw_ref[1:2, :] * win[ROW_HALO:ROW_HALO + rows]
    for k in (0, 2, 3):
        sh = pltpu.roll(win, (1 - k) % total, axis=0)[ROW_HALO:ROW_HALO + rows]
        acc = acc + cw_ref[k:k + 1, :] * sh
    return acc


def _cumsum_rows(a, reverse):
    n = a.shape[0]
    rid = lax.broadcasted_iota(jnp.int32, a.shape, 0)
    k = 1
    while k < n:
        if reverse:
            a = a + jnp.where(rid < n - k, pltpu.roll(a, n - k, axis=0), 0.0)
        else:
            a = a + jnp.where(rid >= k, pltpu.roll(a, k, axis=0), 0.0)
        k *= 2
    return a


def _per_head_cols(cols, width):
    rows = cols[0].shape[0]
    lane = lax.broadcasted_iota(jnp.int32, (rows, len(cols) * width), 1)
    out = jnp.broadcast_to(cols[-1], lane.shape)
    for j in range(len(cols) - 2, -1, -1):
        out = jnp.where(lane < (j + 1) * width, cols[j], out)
    return out


def _ssd_kernel(xbc_c, dt_c, z_c, xbc_l, dt_l, z_l, cw_ref, cb_ref, dtb_ref, aneg_ref, dvec_ref, ng_ref,
                out_c, out_l, xc_c, xc_l, yf_c, yf_l, st_ref, y_ref, *, m_ctx, n_lat):
    seqs = ((xbc_c, dt_c, z_c, out_c, xc_c, yf_c, m_ctx), (xbc_l, dt_l, z_l, out_l, xc_l, yf_l, n_lat))
    heads_per_group = SSD_HEADS // SSD_GROUPS
    gw = heads_per_group * SSD_HEAD_DIM

    for xbc, _, _, _, xc, _, n in seqs:
        def conv_body(j, carry, xbc=xbc, xc=xc, n=n):
            r0 = pl.multiple_of(j * CHUNK, CHUNK)
            win = _conv_window(xbc, n, r0, CHUNK)
            xc[pl.ds(r0, CHUNK), :] = _silu(_dwconv4(win, CHUNK, cw_ref, cb_ref)).astype(BF16)
            return carry
        lax.fori_loop(0, n // CHUNK, conv_body, 0)

    rid = lax.broadcasted_iota(jnp.int32, (CHUNK, CHUNK), 0)
    cid = lax.broadcasted_iota(jnp.int32, (CHUNK, CHUNK), 1)

    def chunk(xc, dtr, r0, d):
        dt = jax.nn.softplus(dtr[0, pl.ds(r0, CHUNK), :] + dtb_ref[...])
        a = dt * aneg_ref[...]
        cs = _cumsum_rows(a, reverse=(d == 1))
        cst = cs.T
        tot = cs[CHUNK - 1:CHUNK, :] if d == 0 else cs[0:1, :]
        mask = (rid >= cid) if d == 0 else (cid >= rid)
        ecs = jnp.exp(cs)
        edec = jnp.exp(tot - cs)
        etot = jnp.exp(tot)
        for g in range(SSD_GROUPS):
            bg = xc[pl.ds(r0, CHUNK), SSD_INNER + g * SSD_STATE:SSD_INNER + (g + 1) * SSD_STATE]
            cg = xc[pl.ds(r0, CHUNK), SSD_INNER + (SSD_GROUPS + g) * SSD_STATE:
                    SSD_INNER + (SSD_GROUPS + g + 1) * SSD_STATE]
            bgt = bg.astype(F32).T.astype(BF16)
            gmat = _dot(cg, bgt)
            xdec = []
            for hl in range(heads_per_group):
                h = g * heads_per_group + hl
                col = d * SSD_HEADS + h
                diff = cs[:, col:col + 1] - cst[col:col + 1, :]
                dec = jnp.exp(jnp.where(mask, diff, -jnp.inf))
                sc = (gmat * dec).astype(BF16)
                xh = xc[pl.ds(r0, CHUNK), h * SSD_HEAD_DIM:(h + 1) * SSD_HEAD_DIM].astype(F32)
                xdt = xh * dt[:, col:col + 1]
                y_ref[:, h * SSD_HEAD_DIM:(h + 1) * SSD_HEAD_DIM] = _dot(sc, xdt.astype(BF16))
                xdec.append((xdt * edec[:, col:col + 1]).astype(BF16))
            c0 = d * SSD_HEADS + g * heads_per_group
            st = st_ref[g]
            yoff = _dot(cg, st.astype(BF16))
            erow = _per_head_cols([ecs[:, c0 + j:c0 + j + 1] for j in range(heads_per_group)], SSD_HEAD_DIM)
            y_ref[:, g * gw:(g + 1) * gw] = y_ref[:, g * gw:(g + 1) * gw] + yoff * erow
            trow = _per_head_cols([etot[:, c0 + j:c0 + j + 1] for j in range(heads_per_group)], SSD_HEAD_DIM)
            xd = jnp.concatenate(xdec, axis=1)
            st_ref[g] = st * trow + _dot(bgt, xd)

    st_ref[...] = jnp.zeros_like(st_ref)
    for _, dtr, _, _, xc, yf, n in seqs:
        def fwd_body(j, carry, dtr=dtr, xc=xc, yf=yf):
            r0 = pl.multiple_of(j * CHUNK, CHUNK)
            chunk(xc, dtr, r0, 0)
            yf[pl.ds(r0, CHUNK), :] = y_ref[...]
            return carry
        lax.fori_loop(0, n // CHUNK, fwd_body, 0)

    st_ref[...] = jnp.zeros_like(st_ref)
    for _, dtr, z, out, xc, yf, n in seqs:
        def bwd_body(j, carry, dtr=dtr, z=z, out=out, xc=xc, yf=yf, n=n):
            r0 = pl.multiple_of((n // CHUNK - 1 - j) * CHUNK, CHUNK)
            chunk(xc, dtr, r0, 1)
            xs = xc[pl.ds(r0, CHUNK), 0:SSD_INNER].astype(F32)
            y = yf[pl.ds(r0, CHUNK), :] + y_ref[...] + xs * dvec_ref[...]
            y = y * _silu(z[0, pl.ds(r0, CHUNK), :].astype(F32))
            gsz = SSD_INNER // SSD_GROUPS
            for g in range(SSD_GROUPS):
                yg = y[:, g * gsz:(g + 1) * gsz]
                ms = jnp.mean(yg * yg, axis=-1, keepdims=True)
                yn = yg * lax.rsqrt(ms + NORM_EPS) * ng_ref[:, g * gsz:(g + 1) * gsz]
                out[0, pl.ds(r0, CHUNK), g * gsz:(g + 1) * gsz] = yn.astype(out.dtype)
            return carry
        lax.fori_loop(0, n // CHUNK, bwd_body, 0)


def _ssd(xbc_c, dt_c, z_c, xbc_l, dt_l, z_l, conv_w, conv_b, dt_bias, a_log, d_skip, norm_g, bsz):
    m_ctx = xbc_c.shape[0] // bsz
    n_lat = xbc_l.shape[0] // bsz
    r3 = lambda a, n: a.reshape(bsz, n, a.shape[-1])
    pad = DT_PAD - 2 * SSD_HEADS
    dtb = jnp.pad(dt_bias.astype(F32).reshape(1, -1), ((0, 0), (0, pad)))
    aneg = jnp.pad(-jnp.exp(a_log.astype(F32)).reshape(1, -1), ((0, 0), (0, pad)))
    dvec = jnp.repeat(d_skip.astype(F32), SSD_HEAD_DIM).reshape(1, SSD_INNER)

    def seq_spec(n, w, single):
        kw = dict(pipeline_mode=pl.Buffered(1)) if single else {}
        return pl.BlockSpec((1, n, w), lambda b: (b, 0, 0), **kw)

    kern = functools.partial(_ssd_kernel, m_ctx=m_ctx, n_lat=n_lat)
    out_c, out_l = pl.pallas_call(
        kern,
        grid=(bsz,),
        in_specs=[seq_spec(m_ctx, SSD_XBC, False), seq_spec(m_ctx, DT_PAD, False), seq_spec(m_ctx, SSD_INNER, False),
                  seq_spec(n_lat, SSD_XBC, True), seq_spec(n_lat, DT_PAD, False), seq_spec(n_lat, SSD_INNER, True),
                  _resident((4, SSD_XBC)), _resident((1, SSD_XBC)), _resident((1, DT_PAD)), _resident((1, DT_PAD)),
                  _resident((1, SSD_INNER)), _resident((1, SSD_INNER))],
        out_specs=[seq_spec(m_ctx, SSD_INNER, False), seq_spec(n_lat, SSD_INNER, False)],
        out_shape=[jax.ShapeDtypeStruct((bsz, m_ctx, SSD_INNER), BF16),
                   jax.ShapeDtypeStruct((bsz, n_lat, SSD_INNER), BF16)],
        scratch_shapes=[pltpu.VMEM((m_ctx, SSD_XBC), BF16), pltpu.VMEM((n_lat, SSD_XBC), BF16),
                        pltpu.VMEM((m_ctx, SSD_INNER), F32), pltpu.VMEM((n_lat, SSD_INNER), F32),
                        pltpu.VMEM((SSD_GROUPS, SSD_STATE, SSD_INNER // SSD_GROUPS), F32),
                        pltpu.VMEM((CHUNK, SSD_INNER), F32)],
        compiler_params=_cparams(1),
        name="ssd",
    )(r3(xbc_c, m_ctx), r3(dt_c, m_ctx), r3(z_c, m_ctx), r3(xbc_l, n_lat), r3(dt_l, n_lat), r3(z_l, n_lat),
      conv_w.astype(F32), conv_b.astype(F32).reshape(1, -1), dtb, aneg, dvec, norm_g.astype(F32).reshape(1, -1))
    return out_c.reshape(bsz * m_ctx, SSD_INNER), out_l.reshape(bsz * n_lat, SSD_INNER)


def _gelu_tanh(x):
    return 0.5 * x * (1.0 + jnp.tanh(math.sqrt(2.0 / math.pi) * (x + 0.044715 * (x * x * x))))


def _rg_kernel(x_c, g_c, x_l, g_l, cw_ref, cb_ref, w_ref, b_ref, lam_ref,
               out_c, out_l, hf_c, hf_l, a_s, b_s, h_s, hcar, *, m_ctx, n_lat):
    seqs = ((x_c, g_c, out_c, hf_c, m_ctx), (x_l, g_l, out_l, hf_l, n_lat))
    gw = 2 * RG_WIDTH

    def chunk(x, n, r0, d):
        win = _conv_window(x, n, r0, CHUNK)
        xr = _dwconv4(win, CHUNK, cw_ref, cb_ref)
        gates = _dot(xr.astype(BF16), w_ref[:, d * gw:(d + 1) * gw]) + b_ref[:, d * gw:(d + 1) * gw]
        sp = jax.nn.softplus(-lam_ref[d:d + 1, :])
        log_a = -RG_C * jax.nn.sigmoid(gates[:, :RG_WIDTH]) * sp
        a = jnp.exp(log_a)
        a_s[...] = a
        one_minus_a2 = -jnp.tanh(log_a) * (a * a + 1.0)
        b_s[...] = jnp.sqrt(one_minus_a2) * (jax.nn.sigmoid(gates[:, RG_WIDTH:]) * xr)

        def step(i, h):
            t = i if d == 0 else CHUNK - 1 - i
            h = a_s[pl.ds(t, 1), :] * h + b_s[pl.ds(t, 1), :]
            h_s[pl.ds(t, 1), :] = h
            return h
        hcar[d:d + 1, :] = lax.fori_loop(0, CHUNK, step, hcar[d:d + 1, :], unroll=8)

    hcar[...] = jnp.zeros_like(hcar)
    for x, _, _, hf, n in seqs:
        def fwd_body(j, carry, x=x, hf=hf, n=n):
            r0 = pl.multiple_of(j * CHUNK, CHUNK)
            chunk(x, n, r0, 0)
            hf[pl.ds(r0, CHUNK), :] = h_s[...]
            return carry
        lax.fori_loop(0, n // CHUNK, fwd_body, 0)

    for x, gg, out, hf, n in seqs:
        def bwd_body(j, carry, x=x, gg=gg, out=out, hf=hf, n=n):
            r0 = pl.multiple_of((n // CHUNK - 1 - j) * CHUNK, CHUNK)
            chunk(x, n, r0, 1)
            hsum = hf[pl.ds(r0, CHUNK), :] + h_s[...]
            gate = _gelu_tanh(gg[0, pl.ds(r0, CHUNK), :].astype(F32))
            out[0, pl.ds(r0, CHUNK), :] = (hsum * gate).astype(out.dtype)
            return carry
        lax.fori_loop(0, n // CHUNK, bwd_body, 0)


def _block_diag(w):
    k, c, e = w.shape
    return jnp.einsum("kce,kj->kcje", w, jnp.eye(k, dtype=w.dtype)).reshape(k * c, k * e)


def _rglru(rgx_c, rgg_c, rgx_l, rgg_l, conv_w, conv_b, wa, ba, wx, bx, lam, bsz):
    m_ctx = rgx_c.shape[0] // bsz
    n_lat = rgx_l.shape[0] // bsz
    r3 = lambda a, n: a.reshape(bsz, n, a.shape[-1])
    w_cat = jnp.concatenate([_block_diag(wa[0]), _block_diag(wx[0]), _block_diag(wa[1]), _block_diag(wx[1])],
                            axis=1).astype(BF16)
    b_cat = jnp.concatenate([ba[0], bx[0], ba[1], bx[1]]).astype(F32).reshape(1, -1)

    def seq_spec(n):
        return pl.BlockSpec((1, n, RG_WIDTH), lambda b: (b, 0, 0))

    kern = functools.partial(_rg_kernel, m_ctx=m_ctx, n_lat=n_lat)
    out_c, out_l = pl.pallas_call(
        kern,
        grid=(bsz,),
        in_specs=[seq_spec(m_ctx), seq_spec(m_ctx), seq_spec(n_lat), seq_spec(n_lat),
                  _resident((4, RG_WIDTH)), _resident((1, RG_WIDTH)), _resident((RG_WIDTH, 4 * RG_WIDTH)),
                  _resident((1, 4 * RG_WIDTH)), _resident((2, RG_WIDTH))],
        out_specs=[seq_spec(m_ctx), seq_spec(n_lat)],
        out_shape=[jax.ShapeDtypeStruct((bsz, m_ctx, RG_WIDTH), BF16),
                   jax.ShapeDtypeStruct((bsz, n_lat, RG_WIDTH), BF16)],
        scratch_shapes=[pltpu.VMEM((m_ctx, RG_WIDTH), F32), pltpu.VMEM((n_lat, RG_WIDTH), F32),
                        pltpu.VMEM((CHUNK, RG_WIDTH), F32), pltpu.VMEM((CHUNK, RG_WIDTH), F32),
                        pltpu.VMEM((CHUNK, RG_WIDTH), F32), pltpu.VMEM((2, RG_WIDTH), F32)],
        compiler_params=_cparams(1),
        name="rglru",
    )(r3(rgx_c, m_ctx), r3(rgg_c, m_ctx), r3(rgx_l, n_lat), r3(rgg_l, n_lat),
      conv_w.astype(F32), conv_b.astype(F32).reshape(1, -1), w_cat, b_cat, lam.astype(F32))
    return out_c.reshape(bsz * m_ctx, RG_WIDTH), out_l.reshape(bsz * n_lat, RG_WIDTH)


def _prep_kernel(x_ref, *rest, width, norm, rope, transpose, scale):
    rest = list(rest)
    g_ref = rest.pop(0) if norm else None
    cos_ref, sin_ref = (rest.pop(0), rest.pop(0)) if rope else (None, None)
    (o_ref,) = rest
    y = x_ref[...].astype(F32)
    if norm:
        sq = y * y
        hi = sq.astype(BF16)
        lo = (sq - hi.astype(F32)).astype(BF16)
        seg_r = lax.broadcasted_iota(jnp.int32, (width, width), 0) // HEAD_DIM
        seg_c = lax.broadcasted_iota(jnp.int32, (width, width), 1) // HEAD_DIM
        ones_bd = jnp.where(seg_r == seg_c, 1.0, 0.0).astype(BF16)
        ss = _dot(hi, ones_bd) + _dot(lo, ones_bd)
        y = y * lax.rsqrt(ss * (1.0 / HEAD_DIM) + NORM_EPS) * g_ref[...]
    if rope:
        cos = cos_ref[...]
        sin = sin_ref[...]
        lane = lax.broadcasted_iota(jnp.int32, (y.shape[0], LANES), 1)
        first = (lane % HEAD_DIM) < (HEAD_DIM // 2)
        cols = []
        for j in range(width // LANES):
            yj = y[:, j * LANES:(j + 1) * LANES]
            partner = jnp.where(first, pltpu.roll(yj, LANES - HEAD_DIM // 2, axis=1),
                                pltpu.roll(yj, HEAD_DIM // 2, axis=1))
            cols.append(yj * cos + partner * sin)
        y = cols[0] if len(cols) == 1 else jnp.concatenate(cols, axis=1)
    if scale != 1.0:
        y = y * scale
    if transpose:
        o_ref[0] = y.T.astype(o_ref.dtype)
    else:
        o_ref[...] = y.astype(o_ref.dtype)


def _prep(x2d, bsz, gain=None, cos=None, sin=None, transpose=False, scale=1.0):
    t, width = x2d.shape
    ntok = t // bsz
    tt = _row_tile(ntok, 512)
    nt = ntok // tt
    norm = gain is not None
    rope = cos is not None
    in_specs = [pl.BlockSpec((tt, width), lambda b, i: (b * nt + i, 0))]
    args = [x2d]
    if norm:
        in_specs.append(_resident((1, width)))
        args.append(jnp.tile(gain.astype(F32), width // HEAD_DIM).reshape(1, width))
    if rope:
        in_specs += [pl.BlockSpec((tt, LANES), lambda b, i: (i, 0))] * 2
        args += [cos, sin]
    if transpose:
        out_spec = pl.BlockSpec((1, width, tt), lambda b, i: (b, 0, i))
        out_shape = jax.ShapeDtypeStruct((bsz, width, ntok), BF16)
    else:
        out_spec = pl.BlockSpec((tt, width), lambda b, i: (b * nt + i, 0))
        out_shape = jax.ShapeDtypeStruct((t, width), BF16)
    kern = functools.partial(_prep_kernel, width=width, norm=norm, rope=rope, transpose=transpose, scale=scale)
    return pl.pallas_call(kern, grid=(bsz, nt), in_specs=in_specs, out_specs=out_spec, out_shape=out_shape,
                          compiler_params=_cparams(2), name="qk_prep")(*args)


KEY_CHUNK = 256
ATTN_COLS = 1024
ONES_ROWS = 16


def _attn_core(qt_ext, k_ref, vt_ref, s_ref, p_ref, m_ref, a_ref, acc_ref, n_keys, tk):
    n_chunks = n_keys // tk
    ones = jnp.ones((ONES_ROWS, tk), BF16)

    def scores(j):
        off = pl.multiple_of(jnp.minimum(j, n_chunks - 1) * tk, tk)
        return _dot(k_ref[0, pl.ds(off, tk), :], qt_ext)

    def softmax(slot):
        s = s_ref[slot]
        m_old = m_ref[...]
        m_new = jnp.maximum(m_old, jnp.max(s, axis=0, keepdims=True))
        a_ref[slot] = jnp.exp2(m_old - m_new)
        p_ref[slot] = jnp.exp2(s - m_new).astype(BF16)
        m_ref[...] = m_new

    def accumulate(j, slot):
        off = pl.multiple_of(j * tk, tk)
        vt = jnp.concatenate([vt_ref[0, :, pl.ds(off, tk)], ones], axis=0)
        acc_ref[...] = a_ref[slot] * acc_ref[...] + _dot(vt, p_ref[slot])

    def stage(j, slot):
        accumulate(j - 1, 1 - slot)
        s_ref[1 - slot] = scores(j + 1)
        softmax(slot)

    m_ref[...] = jnp.full(m_ref.shape, -jnp.inf, F32)
    acc_ref[...] = jnp.zeros(acc_ref.shape, F32)
    s_ref[0] = scores(0)
    s_ref[1] = scores(1)
    softmax(0)

    for j in range(1, n_chunks):
        stage(j, j % 2)
    accumulate(n_chunks - 1, (n_chunks - 1) % 2)


def _attn_scratch(tk, r, dv):
    return [pltpu.VMEM((2, tk, r), F32), pltpu.VMEM((2, tk, r), BF16), pltpu.VMEM((1, r), F32),
            pltpu.VMEM((2, 1, r), F32), pltpu.VMEM((dv + ONES_ROWS, r), F32)]


def _gqa_kernel(qt_ref, k_ref, vt_ref, o_ref, s_ref, p_ref, m_ref, a_ref, acc_ref, *,
                n_keys, tk, tq, rep, kv_heads):
    qt = qt_ref[0]
    qcat = jnp.concatenate([qt[r * HEAD_DIM:(r + 1) * HEAD_DIM, :] for r in range(rep)], axis=1)
    g = pl.program_id(1)
    zero = jnp.zeros_like(qcat)
    qt_ext = jnp.concatenate([jnp.where(g == j, qcat, zero) for j in range(kv_heads)], axis=0)
    _attn_core(qt_ext, k_ref, vt_ref, s_ref, p_ref, m_ref, a_ref, acc_ref, n_keys, tk)
    o = acc_ref[0:HEAD_DIM, :] / acc_ref[HEAD_DIM:HEAD_DIM + 1, :]
    o2 = jnp.concatenate([o[:, r * tq:(r + 1) * tq] for r in range(rep)], axis=0)
    o_ref[...] = o2.T.astype(o_ref.dtype)


def _gqa_attention(qt, k_all, vt_all, n_keys, tk):
    bsz, _, nq = qt.shape
    mtot = k_all.shape[1]
    kv_heads = GQA_KV_WIDTH // HEAD_DIM
    rep = GQA_WIDTH // GQA_KV_WIDTH
    tq = _row_tile(nq, ATTN_COLS // rep)
    nt = nq // tq
    qw = rep * HEAD_DIM
    r = rep * tq
    kern = functools.partial(_gqa_kernel, n_keys=n_keys, tk=tk, tq=tq, rep=rep, kv_heads=kv_heads)
    return pl.pallas_call(
        kern, grid=(bsz, kv_heads, nt),
        in_specs=[pl.BlockSpec((1, qw, tq), lambda b, g, i: (b, g, i)),
                  pl.BlockSpec((1, mtot, GQA_KV_WIDTH), lambda b, g, i: (b, 0, 0)),
                  pl.BlockSpec((1, HEAD_DIM, mtot), lambda b, g, i: (b, g, 0))],
        out_specs=pl.BlockSpec((tq, qw), lambda b, g, i: (b * nt + i, g)),
        out_shape=jax.ShapeDtypeStruct((bsz * nq, GQA_WIDTH), BF16),
        scratch_shapes=_attn_scratch(tk, r, HEAD_DIM),
        compiler_params=_cparams(3), name="gqa_attn")(qt, k_all, vt_all)


def _diff_kernel(qt_ref, k_ref, vt_ref, lam_ref, sg_ref, o_ref, s_ref, p_ref, m_ref, a_ref, acc_ref, *,
                 n_keys, tk, tq, lambda_init):
    qt = qt_ref[0]
    z = jnp.zeros((HEAD_DIM, tq), qt.dtype)
    qt_ext = jnp.concatenate([jnp.concatenate([qt[:HEAD_DIM], z], axis=1),
                              jnp.concatenate([z, qt[HEAD_DIM:]], axis=1)], axis=0)
    _attn_core(qt_ext, k_ref, vt_ref, s_ref, p_ref, m_ref, a_ref, acc_ref, n_keys, tk)
    vw = 2 * HEAD_DIM
    o = acc_ref[0:vw, :] / acc_ref[vw:vw + 1, :]
    lp = lam_ref[...]
    lam = (jnp.exp(jnp.sum(lp[0:1] * lp[1:2], axis=-1, keepdims=True))
           - jnp.exp(jnp.sum(lp[2:3] * lp[3:4], axis=-1, keepdims=True)) + lambda_init)
    o = (o[:, :tq] - lam * o[:, tq:]).T
    ms = jnp.mean(o * o, axis=-1, keepdims=True)
    o = o * lax.rsqrt(ms + NORM_EPS) * sg_ref[...] * (1.0 - lambda_init)
    o_ref[...] = o.astype(o_ref.dtype)


def _diff_attention(qt, k_all, vt_all, lam_p, subln_g, lambda_init, n_keys, tk):
    bsz, _, nq = qt.shape
    mtot = k_all.shape[1]
    tq = _row_tile(nq, ATTN_COLS // 2)
    nt = nq // tq
    vw = 2 * HEAD_DIM
    r = 2 * tq
    kern = functools.partial(_diff_kernel, n_keys=n_keys, tk=tk, tq=tq, lambda_init=lambda_init)
    return pl.pallas_call(
        kern, grid=(bsz, DIFF_HEADS, nt),
        in_specs=[pl.BlockSpec((1, vw, tq), lambda b, h, i: (b, h, i)),
                  pl.BlockSpec((1, mtot, vw), lambda b, h, i: (b, 0, h)),
                  pl.BlockSpec((1, vw, mtot), lambda b, h, i: (b, h, 0)),
                  _resident((4, HEAD_DIM)), _resident((1, vw))],
        out_specs=pl.BlockSpec((tq, vw), lambda b, h, i: (b * nt + i, h)),
        out_shape=jax.ShapeDtypeStruct((bsz * nq, DIFF_WIDTH), BF16),
        scratch_shapes=_attn_scratch(tk, r, vw),
        compiler_params=_cparams(3), name="diff_attn")(qt, k_all, vt_all, lam_p.astype(F32),
                                                       subln_g.astype(F32).reshape(1, vw))


def _merge_kernel(x_ref, g1_ref, gates_ref, o0, o1, o2, o3, wbr_ref, wout_ref, out_ref):
    d = x_ref.shape[1]
    m = None
    for k, o in enumerate((o0, o1, o2, o3)):
        t = gates_ref[:, k * d:(k + 1) * d].astype(F32) * _dot(o[...], wbr_ref[k])
        m = t if m is None else m + t
    y = _dot(m.astype(BF16), wout_ref[...])
    out_ref[...] = x_ref[...] + g1_ref[0] * y


def _merge(x2d, g1, gates, outs, w_br, w_out, rows_per_mod):
    t, d = x2d.shape
    tm = _row_tile(min(t, rows_per_mod), 512)
    row = lambda w: pl.BlockSpec((tm, w), lambda i: (i, 0))
    return pl.pallas_call(
        _merge_kernel,
        grid=(t // tm,),
        in_specs=[row(d), pl.BlockSpec((1, 1, d), lambda i: ((i * tm) // rows_per_mod, 0, 0)), row(N_BRANCHES * d),
                  row(BRANCH_WIDTH), row(BRANCH_WIDTH), row(BRANCH_WIDTH), row(BRANCH_WIDTH),
                  _resident((N_BRANCHES, BRANCH_WIDTH, d)), _resident((d, d))],
        out_specs=row(d),
        out_shape=jax.ShapeDtypeStruct((t, d), F32),
        compiler_params=_cparams(1),
        name="merge",
    )(x2d, g1, gates, *outs, w_br, w_out)


FFN_COL_CHUNK = 256


def _ffn_kernel(x_ref, g2_ref, u_ref, up_ref, un_ref, cw_ref, cb_ref, wd_ref, out_ref, *, tiles_per_seq, d_ff):
    tm = u_ref.shape[0]
    i = pl.program_id(0)
    has_prev = (i % tiles_per_seq) != 0
    has_next = (i % tiles_per_seq) != tiles_per_seq - 1
    rid = lax.broadcasted_iota(jnp.int32, (tm, FFN_COL_CHUNK), 0)

    def conv(c0):
        cur = u_ref[:, c0:c0 + FFN_COL_CHUNK].astype(F32)
        prev_row = jnp.where(has_prev, up_ref[ROW_HALO - 1:ROW_HALO, c0:c0 + FFN_COL_CHUNK].astype(F32), 0.0)
        next_row = jnp.where(has_next, un_ref[0:1, c0:c0 + FFN_COL_CHUNK].astype(F32), 0.0)
        dn = jnp.where(rid == 0, prev_row, pltpu.roll(cur, 1, axis=0))
        up = jnp.where(rid == tm - 1, next_row, pltpu.roll(cur, tm - 1, axis=0))
        w = cw_ref[:, c0:c0 + FFN_COL_CHUNK]
        return w[0:1] * dn + w[1:2] * cur + w[2:3] * up + cb_ref[:, c0:c0 + FFN_COL_CHUNK]

    acc = jnp.zeros((tm, out_ref.shape[1]), F32)
    for c in range(d_ff // FFN_COL_CHUNK):
        c0 = c * FFN_COL_CHUNK
        act = (_silu(conv(c0)) * conv(d_ff + c0)).astype(BF16)
        acc = acc + _dot(act, wd_ref[c0:c0 + FFN_COL_CHUNK, :])
    out_ref[...] = x_ref[...] + g2_ref[0] * acc


def _ffn_down(x2d, g2, u, conv_w, conv_b, w_down, rows_per_mod, seq_len):
    t, d = x2d.shape
    d_ff = w_down.shape[0]
    tm = _row_tile(seq_len, 512)
    hb = tm // ROW_HALO
    n_halo = t // ROW_HALO
    kern = functools.partial(_ffn_kernel, tiles_per_seq=seq_len // tm, d_ff=d_ff)
    return pl.pallas_call(
        kern,
        grid=(t // tm,),
        in_specs=[pl.BlockSpec((tm, d), lambda i: (i, 0)),
                  pl.BlockSpec((1, 1, d), lambda i: ((i * tm) // rows_per_mod, 0, 0)),
                  pl.BlockSpec((tm, 2 * d_ff), lambda i: (i, 0)),
                  pl.BlockSpec((ROW_HALO, 2 * d_ff), lambda i: (jnp.maximum(i * hb - 1, 0), 0)),
                  pl.BlockSpec((ROW_HALO, 2 * d_ff), lambda i: (jnp.minimum((i + 1) * hb, n_halo - 1), 0)),
                  _resident((3, 2 * d_ff)), _resident((1, 2 * d_ff)), _resident((d_ff, d))],
        out_specs=pl.BlockSpec((tm, d), lambda i: (i, 0)),
        out_shape=jax.ShapeDtypeStruct((t, d), F32),
        compiler_params=_cparams(1),
        name="ffn_down",
    )(x2d, g2, u, u, u, conv_w.astype(F32), conv_b.astype(F32).reshape(1, -1), w_down)


def _rope_tables(n_lat):
    t = jnp.arange(n_lat)
    row = (t // GRID_W).astype(F32)
    col = (t % GRID_W).astype(F32)
    n_freq = HEAD_DIM // 4
    inv = ROPE_BASE ** (-jnp.arange(n_freq, dtype=F32) / n_freq)
    ang = jnp.concatenate([row[:, None] * inv, col[:, None] * inv], axis=-1)
    cos, sin = jnp.cos(ang), jnp.sin(ang)
    reps = LANES // HEAD_DIM
    return (jnp.tile(jnp.concatenate([cos, cos], axis=-1), (1, reps)),
            jnp.tile(jnp.concatenate([-sin, sin], axis=-1), (1, reps)))


IN_WIDTHS = (512, 1024, DT_PAD, 512, 128, 128, 512, 512, 512, 512, 512)


def kernel(x, c, ctx, c_ctx, w_ada, b_ada, norm1_g, norm2_g, w_in, ssd_conv_w, ssd_conv_b, ssd_dt_bias, ssd_a_log, ssd_d, ssd_norm_g, gqa_qnorm_g, gqa_knorm_g, diff_qnorm_g, diff_knorm_g, diff_lambda, diff_subln_g, rg_conv_w, rg_conv_b, rg_wa, rg_ba, rg_wx, rg_bx, rg_lambda, w_gate, b_gate, w_br, w_out, w_up, ffn_conv_w, ffn_conv_b, w_down):
    bsz, n_lat, d = x.shape
    m_ctx = ctx.shape[1]
    depth = w_in.shape[0]
    d_ff = w_down.shape[1]
    m_tot = m_ctx + n_lat
    cos, sin = _rope_tables(n_lat)
    cos_all = jnp.concatenate([jnp.ones((m_ctx, LANES), F32), cos], axis=0)
    sin_all = jnp.concatenate([jnp.zeros((m_ctx, LANES), F32), sin], axis=0)
    scale = HEAD_DIM ** -0.5 * math.log2(math.e)
    tk = math.gcd(math.gcd(m_ctx, n_lat), KEY_CHUNK)

    x_l = x.reshape(bsz * n_lat, d)
    x_c = ctx.reshape(bsz * m_ctx, d)
    n_c = bsz + 1
    n_c_pad = -(-n_c // 8) * 8
    cc = jnp.pad(jnp.concatenate([c, c_ctx[None, :]], axis=0), ((0, n_c_pad - n_c), (0, 0)))
    dt_col = IN_SPLITS[0] + IN_SPLITS[1]

    for l in range(depth):
        with_ctx_out = l < depth - 1
        lambda_init = 0.8 - 0.6 * math.exp(-0.3 * l)

        mod = _ada(cc, w_ada[l], b_ada[l])
        mods_l = [mod[:bsz, k * d:(k + 1) * d].reshape(bsz, 1, d) for k in range(6)]
        mods_c = [mod[bsz:bsz + 1, k * d:(k + 1) * d].reshape(1, 1, d) for k in range(6)]

        wi = w_in[l]
        w_cat = jnp.concatenate(
            [wi[:, :dt_col + 16], jnp.zeros((d, DT_PAD - 16), wi.dtype), wi[:, dt_col + 16:],
             jnp.transpose(w_gate[l], (1, 0, 2)).reshape(d, N_BRANCHES * d)], axis=1).astype(BF16)
        b_cat = jnp.concatenate([jnp.zeros((sum(IN_WIDTHS),), F32), b_gate[l].reshape(-1).astype(F32)]).reshape(1, -1)
        widths = IN_WIDTHS + (N_BRANCHES * d,)
        acts = (None,) * len(IN_WIDTHS) + ("sigmoid",)
        dtypes = (BF16, BF16, F32) + (BF16,) * 9

        def in_proj(x2d, mods, rows_per_mod):
            return _proj(x2d, mods[0], mods[1], norm1_g[l], w_cat, b_cat, widths, acts, dtypes, rows_per_mod, 256)

        z_l, xbc_l, dt_l, gq_l, gk_l, gv_l, dq_l, dk_l, dv_l, rgg_l, rgx_l, gates_l = in_proj(x_l, mods_l, n_lat)
        z_c, xbc_c, dt_c, gq_c, gk_c, gv_c, dq_c, dk_c, dv_c, rgg_c, rgx_c, gates_c = in_proj(x_c, mods_c, bsz * m_ctx)

        ssd_c, ssd_l = _ssd(xbc_c, dt_c, z_c, xbc_l, dt_l, z_l, ssd_conv_w[l], ssd_conv_b[l], ssd_dt_bias[l],
                            ssd_a_log[l], ssd_d[l], ssd_norm_g[l], bsz)
        rg_c, rg_l = _rglru(rgx_c, rgg_c, rgx_l, rgg_l, rg_conv_w[l], rg_conv_b[l], rg_wa[l], rg_ba[l],
                            rg_wx[l], rg_bx[l], rg_lambda[l], bsz)

        def keys_values(k_c, k_l, v_c, v_l, knorm_g):
            w = k_c.shape[-1]
            k_raw = jnp.concatenate([k_c.reshape(bsz, m_ctx, w), k_l.reshape(bsz, n_lat, w)], axis=1)
            v_raw = jnp.concatenate([v_c.reshape(bsz, m_ctx, w), v_l.reshape(bsz, n_lat, w)], axis=1)
            k_all = _prep(k_raw.reshape(bsz * m_tot, w), bsz, knorm_g, cos_all, sin_all)
            vt_all = _prep(v_raw.reshape(bsz * m_tot, w), bsz, transpose=True)
            return k_all.reshape(bsz, m_tot, w), vt_all

        gk_all, gvt_all = keys_values(gk_c, gk_l, gv_c, gv_l, gqa_knorm_g[l])
        gqt_l = _prep(gq_l, bsz, gqa_qnorm_g[l], cos, sin, transpose=True, scale=scale)
        gqa_l = _gqa_attention(gqt_l, gk_all, gvt_all, m_tot, tk)
        dk_all, dvt_all = keys_values(dk_c, dk_l, dv_c, dv_l, diff_knorm_g[l])
        dqt_l = _prep(dq_l, bsz, diff_qnorm_g[l], cos, sin, transpose=True, scale=scale)
        diff_l = _diff_attention(dqt_l, dk_all, dvt_all, diff_lambda[l], diff_subln_g[l], lambda_init, m_tot, tk)

        wbr = w_br[l].astype(BF16)
        wout = w_out[l].astype(BF16)
        wup = w_up[l].astype(BF16)
        wdn = w_down[l].astype(BF16)
        b_up = jnp.zeros((1, 2 * d_ff), F32)

        def ffn(x2d, mods, rows_per_mod, seq_len):
            (u,) = _proj(x2d, mods[3], mods[4], norm2_g[l], wup, b_up, (2 * d_ff,), (None,), (BF16,),
                         rows_per_mod, 256)
            return _ffn_down(x2d, mods[5], u, ffn_conv_w[l], ffn_conv_b[l], wdn, rows_per_mod, seq_len)

        x_l = _merge(x_l, mods_l[2], gates_l, (ssd_l, gqa_l, diff_l, rg_l), wbr, wout, n_lat)
        x_l = ffn(x_l, mods_l, n_lat, n_lat)
        if with_ctx_out:
            gqt_c = _prep(gq_c, bsz, gqa_qnorm_g[l], transpose=True, scale=scale)
            gqa_c = _gqa_attention(gqt_c, gk_all, gvt_all, m_ctx, tk)
            dqt_c = _prep(dq_c, bsz, diff_qnorm_g[l], transpose=True, scale=scale)
            diff_c = _diff_attention(dqt_c, dk_all, dvt_all, diff_lambda[l], diff_subln_g[l], lambda_init, m_ctx, tk)
            x_c = _merge(x_c, mods_c[2], gates_c, (ssd_c, gqa_c, diff_c, rg_c), wbr, wout, bsz * m_ctx)
            x_c = ffn(x_c, mods_c, bsz * m_ctx, m_ctx)
    return x_l.reshape(bsz, n_lat, d)
```

```python
import functools
import math

import jax
import jax.numpy as jnp
from jax import lax
from jax.experimental import pallas as pl
from jax.experimental.pallas import tpu as pltpu

F32 = jnp.float32
BF16 = jnp.bfloat16

GRID_W = 64
HEAD_DIM = 64
ROPE_BASE = 10000.0
NORM_EPS = 1e-6
SSD_HEADS = 8
SSD_HEAD_DIM = 64
SSD_INNER = 512
SSD_GROUPS = 2
SSD_STATE = 128
SSD_XBC = 1024
GQA_WIDTH = 512
GQA_KV_WIDTH = 128
DIFF_HEADS = 4
DIFF_WIDTH = 512
RG_WIDTH = 512
RG_C = 8.0
BRANCH_WIDTH = 512
N_BRANCHES = 4
IN_SPLITS = (512, 1024, 16, 512, 128, 128, 512, 512, 512, 512, 512)
DT_PAD = 128

VMEM_LIMIT_BYTES = 56 * 1024 * 1024
LANES = 128
F32_SUBLANES = 8
ROW_HALO = 16
CHUNK = 128
TOKEN_TILE = 256
WIDE_TILE = 512


def _cparams(n):
    return pltpu.CompilerParams(dimension_semantics=("parallel",) * n, vmem_limit_bytes=VMEM_LIMIT_BYTES)


def _resident(shape):
    nd = len(shape)
    return pl.BlockSpec(shape, lambda *_: (0,) * nd, pipeline_mode=pl.Buffered(1))


def _sigmoid(x):
    return 0.5 * jnp.tanh(0.5 * x) + 0.5


def _silu(x):
    h = 0.5 * x
    return h + h * jnp.tanh(h)


def _dot(a, b):
    return jnp.dot(a, b, preferred_element_type=F32)


def _row_tile(n, want):
    t = min(want, n)
    while n % t:
        t //= 2
    return t


def _rms_mod(xf, gain, shift, scale):
    ms = jnp.mean(xf * xf, axis=-1, keepdims=True)
    return xf * lax.rsqrt(ms + NORM_EPS) * gain * (1.0 + scale) + shift


def _ada_kernel(c_ref, w_ref, b_ref, o_ref):
    h = _silu(c_ref[...]).astype(BF16)
    o_ref[...] = _dot(h, w_ref[...].astype(BF16)) + b_ref[...]


def _ada(cc, w, b):
    r, d = cc.shape
    n = w.shape[1]
    tn = _row_tile(n, 1536)
    return pl.pallas_call(
        _ada_kernel,
        grid=(n // tn,),
        in_specs=[pl.BlockSpec((r, d), lambda j: (0, 0)),
                  pl.BlockSpec((d, tn), lambda j: (0, j)),
                  pl.BlockSpec((1, tn), lambda j: (0, j))],
        out_specs=pl.BlockSpec((r, tn), lambda j: (0, j)),
        out_shape=jax.ShapeDtypeStruct((r, n), F32),
        compiler_params=_cparams(1),
        name="ada",
    )(cc, w, b.reshape(1, n))


def _proj_kernel(x_ref, sh_ref, sc_ref, g_ref, w_ref, b_ref, pg_ref, ones_ref, cos_ref, sin_ref, *out_refs, specs):
    h = _rms_mod(x_ref[0], g_ref[...], sh_ref[0], sc_ref[0]).astype(BF16)
    tm = h.shape[0]
    lane = lax.broadcasted_iota(jnp.int32, (tm, LANES), 1)
    first_half = (lane % HEAD_DIM) < (HEAD_DIM // 2)
    c0 = 0
    for o_ref, (w, kind, transpose, scale) in zip(out_refs, specs):
        if kind in ("plain", "sigmoid") and not transpose:
            for s in range(0, w, 512):
                e = min(s + 512, w)
                acc = _dot(h, w_ref[:, c0 + s:c0 + e])
                if kind == "sigmoid":
                    acc = _sigmoid(acc + b_ref[:, c0 + s:c0 + e])
                o_ref[0, :, s:e] = acc.astype(o_ref.dtype)
        else:
            y = _dot(h, w_ref[:, c0:c0 + w])
            if kind == "qk":
                ss = _dot((y * y).astype(BF16), ones_ref[0:w, 0:w])
                y = y * lax.rsqrt(ss * (1.0 / HEAD_DIM) + NORM_EPS) * pg_ref[:, c0:c0 + w]
                cos = cos_ref[...]
                sin = sin_ref[...]
                cols = []
                for j in range(w // LANES):
                    yj = y[:, j * LANES:(j + 1) * LANES]
                    partner = jnp.where(first_half, pltpu.roll(yj, LANES - HEAD_DIM // 2, axis=1),
                                        pltpu.roll(yj, HEAD_DIM // 2, axis=1))
                    cols.append(yj * cos + partner * sin)
                y = cols[0] if len(cols) == 1 else jnp.concatenate(cols, axis=1)
                if scale != 1.0:
                    y = y * scale
            if transpose:
                o_ref[0] = y.T.astype(o_ref.dtype)
            else:
                o_ref[0] = y.astype(o_ref.dtype)
        c0 += w


def _in_proj(x3, mods, gain, w_cat, b_cat, post_gain, ones_bd, cos_all, sin_all, specs, dtypes, tm, lat_tiles):
    bsz, m_tot, d = x3.shape
    tpb = m_tot // tm
    ncat = w_cat.shape[1]

    def mod_idx(i):
        return (2 * (i // tpb) + jnp.where(i % tpb >= lat_tiles, 1, 0), 0, 0)

    out_specs, out_shapes = [], []
    for (w, _, transpose, _), dt in zip(specs, dtypes):
        if transpose:
            out_specs.append(pl.BlockSpec((1, w, tm), lambda i: (i // tpb, 0, i % tpb)))
            out_shapes.append(jax.ShapeDtypeStruct((bsz, w, m_tot), dt))
        else:
            out_specs.append(pl.BlockSpec((1, tm, w), lambda i: (i // tpb, i % tpb, 0)))
            out_shapes.append(jax.ShapeDtypeStruct((bsz, m_tot, w), dt))
    kern = functools.partial(_proj_kernel, specs=tuple(specs))
    return pl.pallas_call(
        kern,
        grid=(bsz * tpb,),
        in_specs=[pl.BlockSpec((1, tm, d), lambda i: (i // tpb, i % tpb, 0)),
                  pl.BlockSpec((1, 1, d), mod_idx), pl.BlockSpec((1, 1, d), mod_idx),
                  _resident((1, d)), _resident((d, ncat)), _resident((1, ncat)), _resident((1, ncat)),
                  _resident(ones_bd.shape),
                  pl.BlockSpec((tm, LANES), lambda i: (i % tpb, 0)),
                  pl.BlockSpec((tm, LANES), lambda i: (i % tpb, 0))],
        out_specs=out_specs,
        out_shape=out_shapes,
        compiler_params=_cparams(1),
        name="in_proj",
    )(x3, mods[0], mods[1], gain.reshape(1, d), w_cat, b_cat, post_gain, ones_bd, cos_all, sin_all)


def _conv_window(src, off, n, r0, rows):
    cur = src[0, pl.ds(r0, rows), :].astype(F32)
    lo = pl.multiple_of(jnp.maximum(r0 - ROW_HALO, off), ROW_HALO)
    hi = pl.multiple_of(jnp.minimum(r0 + rows, off + n - ROW_HALO), ROW_HALO)
    prev = src[0, pl.ds(lo, ROW_HALO), :].astype(F32)
    nxt = src[0, pl.ds(hi, ROW_HALO), :].astype(F32)
    prev = jnp.where(r0 > off, prev, 0.0)
    nxt = jnp.where(r0 + rows < off + n, nxt, 0.0)
    return jnp.concatenate([prev, cur, nxt], axis=0)


def _dwconv4(win, rows, cw_ref, cb_ref):
    total = rows + 2 * ROW_HALO
    acc = cb_ref[...] + cw_ref[1:2, :] * win[ROW_HALO:ROW_HALO + rows]
    for k in (0, 2, 3):
        sh = pltpu.roll(win, (1 - k) % total, axis=0)[ROW_HALO:ROW_HALO + rows]
        acc = acc + cw_ref[k:k + 1, :] * sh
    return acc


def _cumsum_rows(a, reverse):
    n = a.shape[0]
    rid = lax.broadcasted_iota(jnp.int32, a.shape, 0)
    k = 1
    while k < n:
        if reverse:
            a = a + jnp.where(rid < n - k, pltpu.roll(a, n - k, axis=0), 0.0)
        else:
            a = a + jnp.where(rid >= k, pltpu.roll(a, k, axis=0), 0.0)
        k *= 2
    return a


def _ssd_kernel(xbc, dtr, z, cw_ref, cb_ref, dtb_ref, aneg_ref, dvec_ref, ng_ref, sel_ref,
                out, xc, yf, st_ref, y_ref, *, segs):
    heads_per_group = SSD_HEADS // SSD_GROUPS
    gw = heads_per_group * SSD_HEAD_DIM
    pair_w = 2 * SSD_HEAD_DIM

    for off, n in segs:
        def conv_body(j, carry, off=off, n=n):
            r0 = pl.multiple_of(off + j * CHUNK, CHUNK)
            win = _conv_window(xbc, off, n, r0, CHUNK)
            xc[pl.ds(r0, CHUNK), :] = _silu(_dwconv4(win, CHUNK, cw_ref, cb_ref)).astype(BF16)
            return carry
        lax.fori_loop(0, n // CHUNK, conv_body, 0)

    rid = lax.broadcasted_iota(jnp.int32, (CHUNK, CHUNK), 0)
    cid = lax.broadcasted_iota(jnp.int32, (CHUNK, CHUNK), 1)
    low_half = cid < SSD_HEAD_DIM

    def chunk(r0, d):
        dt = jax.nn.softplus(dtr[0, pl.ds(r0, CHUNK), :] + dtb_ref[...])
        cs = _cumsum_rows(dt * aneg_ref[...], reverse=(d == 1))
        cst = cs.T
        tot = cs[CHUNK - 1:CHUNK, :] if d == 0 else cs[0:1, :]
        mask = (rid >= cid) if d == 0 else (cid >= rid)
        etot = jnp.exp(tot)
        et_hi = etot.astype(BF16).astype(F32)
        rows8 = lambda v: jnp.broadcast_to(v, (F32_SUBLANES, v.shape[1]))
        stack = jnp.concatenate([dt, dt * jnp.exp(tot - cs), jnp.exp(cs), rows8(et_hi), rows8(etot - et_hi)],
                                axis=0).astype(BF16)
        spread = _dot(stack, sel_ref[d])
        dt_b = spread[0:CHUNK]
        w_b = spread[CHUNK:2 * CHUNK]
        e_b = spread[2 * CHUNK:3 * CHUNK]
        t_b = spread[3 * CHUNK:3 * CHUNK + 1] + spread[3 * CHUNK + F32_SUBLANES:3 * CHUNK + F32_SUBLANES + 1]
        xs = xc[pl.ds(r0, CHUNK), 0:SSD_INNER].astype(F32)
        xdt = xs * dt_b
        xdec = (xs * w_b).astype(BF16)
        for g in range(SSD_GROUPS):
            bg = xc[pl.ds(r0, CHUNK), SSD_INNER + g * SSD_STATE:SSD_INNER + (g + 1) * SSD_STATE]
            cg = xc[pl.ds(r0, CHUNK), SSD_INNER + (SSD_GROUPS + g) * SSD_STATE:
                    SSD_INNER + (SSD_GROUPS + g + 1) * SSD_STATE]
            bgt = bg.astype(F32).T.astype(BF16)
            gmat = _dot(cg, bgt)
            ypairs = []
            for q in range(heads_per_group // 2):
                scs = []
                for hl in (2 * q, 2 * q + 1):
                    col = d * SSD_HEADS + g * heads_per_group + hl
                    diff = cs[:, col:col + 1] - cst[col:col + 1, :]
                    scs.append((gmat * jnp.exp(jnp.where(mask, diff, -jnp.inf))).astype(BF16))
                c0 = g * gw + q * pair_w
                xp = xdt[:, c0:c0 + pair_w]
                rhs = jnp.concatenate([jnp.where(low_half, xp, 0.0).astype(BF16),
                                       jnp.where(low_half, 0.0, xp).astype(BF16)], axis=0)
                ypairs.append(_dot(jnp.concatenate(scs, axis=1), rhs))
            st = st_ref[g]
            yoff = _dot(cg, st.astype(BF16))
            y_ref[:, g * gw:(g + 1) * gw] = jnp.concatenate(ypairs, axis=1) + yoff * e_b[:, g * gw:(g + 1) * gw]
            st_ref[g] = st * t_b[:, g * gw:(g + 1) * gw] + _dot(bgt, xdec[:, g * gw:(g + 1) * gw])

    st_ref[...] = jnp.zeros_like(st_ref)
    for off, n in segs:
        def fwd_body(j, carry, off=off):
            r0 = pl.multiple_of(off + j * CHUNK, CHUNK)
            chunk(r0, 0)
            yf[pl.ds(r0, CHUNK), :] = y_ref[...]
            return carry
        lax.fori_loop(0, n // CHUNK, fwd_body, 0)

    st_ref[...] = jnp.zeros_like(st_ref)
    for off, n in segs:
        def bwd_body(j, carry, off=off, n=n):
            r0 = pl.multiple_of(off + (n // CHUNK - 1 - j) * CHUNK, CHUNK)
            chunk(r0, 1)
            xs = xc[pl.ds(r0, CHUNK), 0:SSD_INNER].astype(F32)
            y = yf[pl.ds(r0, CHUNK), :] + y_ref[...] + xs * dvec_ref[...]
            y = y * _silu(z[0, pl.ds(r0, CHUNK), :].astype(F32))
            gsz = SSD_INNER // SSD_GROUPS
            for g in range(SSD_GROUPS):
                yg = y[:, g * gsz:(g + 1) * gsz]
                ms = jnp.mean(yg * yg, axis=-1, keepdims=True)
                yn = yg * lax.rsqrt(ms + NORM_EPS) * ng_ref[:, g * gsz:(g + 1) * gsz]
                out[0, pl.ds(r0, CHUNK), g * gsz:(g + 1) * gsz] = yn.astype(out.dtype)
            return carry
        lax.fori_loop(0, n // CHUNK, bwd_body, 0)


def _ssd(xbc, dt_raw, z, conv_w, conv_b, dt_bias, a_log, d_skip, norm_g, segs):
    bsz, m_tot, _ = xbc.shape
    pad = DT_PAD - 2 * SSD_HEADS
    dtb = jnp.pad(dt_bias.astype(F32).reshape(1, -1), ((0, 0), (0, pad)))
    aneg = jnp.pad(-jnp.exp(a_log.astype(F32)).reshape(1, -1), ((0, 0), (0, pad)))
    dvec = jnp.repeat(d_skip.astype(F32), SSD_HEAD_DIM).reshape(1, SSD_INNER)
    col = jnp.arange(DT_PAD)[None, :, None]
    head = (jnp.arange(SSD_INNER) // SSD_HEAD_DIM)[None, None, :]
    sel = (col == jnp.arange(2)[:, None, None] * SSD_HEADS + head).astype(BF16)

    def seq_spec(w, single):
        kw = dict(pipeline_mode=pl.Buffered(1)) if single else {}
        return pl.BlockSpec((1, m_tot, w), lambda b: (b, 0, 0), **kw)

    kern = functools.partial(_ssd_kernel, segs=segs)
    return pl.pallas_call(
        kern,
        grid=(bsz,),
        in_specs=[seq_spec(SSD_XBC, True), seq_spec(DT_PAD, True), seq_spec(SSD_INNER, True),
                  _resident((4, SSD_XBC)), _resident((1, SSD_XBC)), _resident((1, DT_PAD)), _resident((1, DT_PAD)),
                  _resident((1, SSD_INNER)), _resident((1, SSD_INNER)), _resident((2, DT_PAD, SSD_INNER))],
        out_specs=seq_spec(SSD_INNER, False),
        out_shape=jax.ShapeDtypeStruct((bsz, m_tot, SSD_INNER), BF16),
        scratch_shapes=[pltpu.VMEM((m_tot, SSD_XBC), BF16), pltpu.VMEM((m_tot, SSD_INNER), F32),
                        pltpu.VMEM((SSD_GROUPS, SSD_STATE, SSD_INNER // SSD_GROUPS), F32),
                        pltpu.VMEM((CHUNK, SSD_INNER), F32)],
        compiler_params=_cparams(1),
        name="ssd",
    )(xbc, dt_raw, z, conv_w.astype(F32), conv_b.astype(F32).reshape(1, -1), dtb, aneg, dvec,
      norm_g.astype(F32).reshape(1, -1), sel)


def _gelu_tanh(x):
    return 0.5 * x * (1.0 + jnp.tanh(math.sqrt(2.0 / math.pi) * (x + 0.044715 * (x * x * x))))


def _rg_kernel(x, gg, cw_ref, cb_ref, w_ref, b_ref, lam_ref, out, hf, a_s, b_s, h_s, hcar, *, segs):
    gw = 2 * RG_WIDTH

    def chunk(off, n, r0, d):
        win = _conv_window(x, off, n, r0, CHUNK)
        xr = _dwconv4(win, CHUNK, cw_ref, cb_ref)
        gates = _dot(xr.astype(BF16), w_ref[:, d * gw:(d + 1) * gw]) + b_ref[:, d * gw:(d + 1) * gw]
        sp = jax.nn.softplus(-lam_ref[d:d + 1, :])
        log_a = -RG_C * _sigmoid(gates[:, :RG_WIDTH]) * sp
        a = jnp.exp(log_a)
        a_s[...] = a
        one_minus_a2 = -jnp.tanh(log_a) * (a * a + 1.0)
        b_s[...] = jnp.sqrt(one_minus_a2) * (_sigmoid(gates[:, RG_WIDTH:]) * xr)

        def step(i, h):
            t = i if d == 0 else CHUNK - 1 - i
            h = a_s[pl.ds(t, 1), :] * h + b_s[pl.ds(t, 1), :]
            h_s[pl.ds(t, 1), :] = h
            return h
        hcar[d:d + 1, :] = lax.fori_loop(0, CHUNK, step, hcar[d:d + 1, :], unroll=8)

    hcar[...] = jnp.zeros_like(hcar)
    for off, n in segs:
        def fwd_body(j, carry, off=off, n=n):
            r0 = pl.multiple_of(off + j * CHUNK, CHUNK)
            chunk(off, n, r0, 0)
            hf[pl.ds(r0, CHUNK), :] = h_s[...]
            return carry
        lax.fori_loop(0, n // CHUNK, fwd_body, 0)

    for off, n in segs:
        def bwd_body(j, carry, off=off, n=n):
            r0 = pl.multiple_of(off + (n // CHUNK - 1 - j) * CHUNK, CHUNK)
            chunk(off, n, r0, 1)
            hsum = hf[pl.ds(r0, CHUNK), :] + h_s[...]
            gate = _gelu_tanh(gg[0, pl.ds(r0, CHUNK), :].astype(F32))
            out[0, pl.ds(r0, CHUNK), :] = (hsum * gate).astype(out.dtype)
            return carry
        lax.fori_loop(0, n // CHUNK, bwd_body, 0)


def _block_diag(w):
    k, c, e = w.shape
    return jnp.einsum("kce,kj->kcje", w, jnp.eye(k, dtype=w.dtype)).reshape(k * c, k * e)


def _rglru(rgx, rgg, conv_w, conv_b, wa, ba, wx, bx, lam, segs):
    bsz, m_tot, _ = rgx.shape
    w_cat = jnp.concatenate([_block_diag(wa[0]), _block_diag(wx[0]), _block_diag(wa[1]), _block_diag(wx[1])],
                            axis=1).astype(BF16)
    b_cat = jnp.concatenate([ba[0], bx[0], ba[1], bx[1]]).astype(F32).reshape(1, -1)
    seq_spec = pl.BlockSpec((1, m_tot, RG_WIDTH), lambda b: (b, 0, 0))
    kern = functools.partial(_rg_kernel, segs=segs)
    return pl.pallas_call(
        kern,
        grid=(bsz,),
        in_specs=[seq_spec, seq_spec,
                  _resident((4, RG_WIDTH)), _resident((1, RG_WIDTH)), _resident((RG_WIDTH, 4 * RG_WIDTH)),
                  _resident((1, 4 * RG_WIDTH)), _resident((2, RG_WIDTH))],
        out_specs=seq_spec,
        out_shape=jax.ShapeDtypeStruct((bsz, m_tot, RG_WIDTH), BF16),
        scratch_shapes=[pltpu.VMEM((m_tot, RG_WIDTH), F32),
                        pltpu.VMEM((CHUNK, RG_WIDTH), F32), pltpu.VMEM((CHUNK, RG_WIDTH), F32),
                        pltpu.VMEM((CHUNK, RG_WIDTH), F32), pltpu.VMEM((2, RG_WIDTH), F32)],
        compiler_params=_cparams(1),
        name="rglru",
    )(rgx, rgg, conv_w.astype(F32), conv_b.astype(F32).reshape(1, -1), w_cat, b_cat, lam.astype(F32))


KEY_CHUNK = 256
ATTN_COLS = 1024
ONES_ROWS = 16


def _attn_core(qt_ext, k_ref, vt_ref, scratch, key_off, n_keys, tk):
    s_refs, p_refs, acc_ref = scratch[0:2], scratch[2:4], scratch[4]
    chunks = [(key_off + o, min(tk, n_keys - o)) for o in range(0, n_keys, tk)]
    n_chunks = len(chunks)

    def scores(j, slot):
        off, size = chunks[min(j, n_chunks - 1)]
        s = _dot(k_ref[0, off:off + size, :], qt_ext)
        s_refs[slot][0:size, :] = s
        return jnp.max(s, axis=0, keepdims=True)

    def softmax(j, slot, m_old, cmax):
        size = chunks[j][1]
        m_new = jnp.maximum(m_old, cmax)
        p_refs[slot][0:size, :] = jnp.exp2((s_refs[slot][0:size, :] - m_new).astype(BF16))
        return m_new, jnp.exp2(m_old - m_new)

    def accumulate(j, slot, alpha):
        off, size = chunks[j]
        vt = jnp.concatenate([vt_ref[0, :, off:off + size], jnp.ones((ONES_ROWS, size), BF16)], axis=0)
        acc_ref[...] = alpha * acc_ref[...] + _dot(vt, p_refs[slot][0:size, :])

    acc_ref[...] = jnp.zeros(acc_ref.shape, F32)
    cmax = [scores(0, 0), scores(1, 1)]
    alpha = [None, None]
    m, alpha[0] = softmax(0, 0, jnp.full(cmax[0].shape, -jnp.inf, F32), cmax[0])
    for j in range(1, n_chunks):
        slot = j % 2
        accumulate(j - 1, 1 - slot, alpha[1 - slot])
        cmax_next = scores(j + 1, 1 - slot)
        m, alpha[slot] = softmax(j, slot, m, cmax[slot])
        cmax[1 - slot] = cmax_next
    last = (n_chunks - 1) % 2
    accumulate(n_chunks - 1, last, alpha[last])


N_ATTN_SCRATCH = 5


def _attn_scratch(tk, r, dv):
    return [pltpu.VMEM((tk, r), F32), pltpu.VMEM((tk, r), F32), pltpu.VMEM((tk, r), BF16),
            pltpu.VMEM((tk, r), BF16), pltpu.VMEM((dv + ONES_ROWS, r), F32)]


def _gqa_kernel(qt_ref, k_ref, vt_ref, *rest, key_off, n_keys, tk, tq, rep, kv_heads):
    o_ref, scratch = rest[-N_ATTN_SCRATCH - 1], rest[-N_ATTN_SCRATCH:]
    acc_ref = scratch[-1]
    qt = qt_ref[0]
    qcat = jnp.concatenate([qt[r * HEAD_DIM:(r + 1) * HEAD_DIM, :] for r in range(rep)], axis=1)
    g = pl.program_id(1)
    zero = jnp.zeros_like(qcat)
    qt_ext = jnp.concatenate([jnp.where(g == j, qcat, zero) for j in range(kv_heads)], axis=0)
    _attn_core(qt_ext, k_ref, vt_ref, scratch, key_off, n_keys, tk)
    o = acc_ref[0:HEAD_DIM, :] / acc_ref[HEAD_DIM:HEAD_DIM + 1, :]
    o2 = jnp.concatenate([o[:, r * tq:(r + 1) * tq] for r in range(rep)], axis=0)
    o_ref[0] = o2.T.astype(o_ref.dtype)


def _alias_args(prev_out, n_inputs):
    if prev_out is None:
        return [], [], {}
    return [pl.BlockSpec(memory_space=pl.ANY)], [prev_out], {n_inputs: 0}


def _gqa_attention(qt, k_all, vt_all, *, q_off, n_q, key_off, n_keys, tk, prev_out=None):
    bsz, _, m_tot = qt.shape
    kv_heads = GQA_KV_WIDTH // HEAD_DIM
    rep = GQA_WIDTH // GQA_KV_WIDTH
    tq = _row_tile(math.gcd(n_q, q_off) if q_off else n_q, ATTN_COLS // rep)
    qb0 = q_off // tq
    qw = rep * HEAD_DIM
    r = rep * tq
    alias_specs, alias_args, aliases = _alias_args(prev_out, 3)
    kern = functools.partial(_gqa_kernel, key_off=key_off, n_keys=n_keys, tk=tk, tq=tq, rep=rep, kv_heads=kv_heads)
    return pl.pallas_call(
        kern, grid=(bsz, kv_heads, n_q // tq),
        in_specs=[pl.BlockSpec((1, qw, tq), lambda b, g, i: (b, g, qb0 + i)),
                  pl.BlockSpec((1, m_tot, GQA_KV_WIDTH), lambda b, g, i: (b, 0, 0)),
                  pl.BlockSpec((1, HEAD_DIM, m_tot), lambda b, g, i: (b, g, 0))] + alias_specs,
        out_specs=pl.BlockSpec((1, tq, qw), lambda b, g, i: (b, qb0 + i, g)),
        out_shape=jax.ShapeDtypeStruct((bsz, m_tot, GQA_WIDTH), BF16),
        scratch_shapes=_attn_scratch(tk, r, HEAD_DIM),
        input_output_aliases=aliases,
        compiler_params=_cparams(3), name="gqa_attn")(qt, k_all, vt_all, *alias_args)


def _diff_kernel(qt_ref, k_ref, vt_ref, lam_ref, sg_ref, *rest, key_off, n_keys, tk, tq, lambda_init):
    o_ref, scratch = rest[-N_ATTN_SCRATCH - 1], rest[-N_ATTN_SCRATCH:]
    acc_ref = scratch[-1]
    qt = qt_ref[0]
    z = jnp.zeros((HEAD_DIM, tq), qt.dtype)
    qt_ext = jnp.concatenate([jnp.concatenate([qt[:HEAD_DIM], z], axis=1),
                              jnp.concatenate([z, qt[HEAD_DIM:]], axis=1)], axis=0)
    _attn_core(qt_ext, k_ref, vt_ref, scratch, key_off, n_keys, tk)
    vw = 2 * HEAD_DIM
    o = acc_ref[0:vw, :] / acc_ref[vw:vw + 1, :]
    lp = lam_ref[...]
    lam = (jnp.exp(jnp.sum(lp[0:1] * lp[1:2], axis=-1, keepdims=True))
           - jnp.exp(jnp.sum(lp[2:3] * lp[3:4], axis=-1, keepdims=True)) + lambda_init)
    o = (o[:, :tq] - lam * o[:, tq:]).T
    ms = jnp.mean(o * o, axis=-1, keepdims=True)
    o = o * lax.rsqrt(ms + NORM_EPS) * sg_ref[...] * (1.0 - lambda_init)
    o_ref[0] = o.astype(o_ref.dtype)


def _diff_attention(qt, k_all, vt_all, lam_p, subln_g, lambda_init, *, q_off, n_q, key_off, n_keys, tk,
                    prev_out=None):
    bsz, _, m_tot = qt.shape
    tq = _row_tile(math.gcd(n_q, q_off) if q_off else n_q, ATTN_COLS // 2)
    qb0 = q_off // tq
    vw = 2 * HEAD_DIM
    r = 2 * tq
    alias_specs, alias_args, aliases = _alias_args(prev_out, 5)
    kern = functools.partial(_diff_kernel, key_off=key_off, n_keys=n_keys, tk=tk, tq=tq, lambda_init=lambda_init)
    return pl.pallas_call(
        kern, grid=(bsz, DIFF_HEADS, n_q // tq),
        in_specs=[pl.BlockSpec((1, vw, tq), lambda b, h, i: (b, h, qb0 + i)),
                  pl.BlockSpec((1, m_tot, vw), lambda b, h, i: (b, 0, h)),
                  pl.BlockSpec((1, vw, m_tot), lambda b, h, i: (b, h, 0)),
                  _resident((4, HEAD_DIM)), _resident((1, vw))] + alias_specs,
        out_specs=pl.BlockSpec((1, tq, vw), lambda b, h, i: (b, qb0 + i, h)),
        out_shape=jax.ShapeDtypeStruct((bsz, m_tot, DIFF_WIDTH), BF16),
        scratch_shapes=_attn_scratch(tk, r, vw),
        input_output_aliases=aliases,
        compiler_params=_cparams(3), name="diff_attn")(qt, k_all, vt_all, lam_p.astype(F32),
                                                       subln_g.astype(F32).reshape(1, vw), *alias_args)


def _merge_kernel(x_ref, g1_ref, gates_ref, o0, o1, o2, o3, wbr_ref, wout_ref, out_ref):
    d = x_ref.shape[2]
    m = None
    for k, o in enumerate((o0, o1, o2, o3)):
        t = gates_ref[0, :, k * d:(k + 1) * d].astype(F32) * _dot(o[0], wbr_ref[k])
        m = t if m is None else m + t
    y = _dot(m.astype(BF16), wout_ref[...])
    out_ref[0] = x_ref[0] + g1_ref[0] * y


def _merge(x3, g1, gates, outs, w_br, w_out, tm, n_tiles, ctx_from):
    bsz, m_tot, d = x3.shape
    row = lambda w: pl.BlockSpec((1, tm, w), lambda b, i: (b, i, 0))
    return pl.pallas_call(
        _merge_kernel,
        grid=(bsz, n_tiles),
        in_specs=[row(d), pl.BlockSpec((1, 1, d), lambda b, i: (2 * b + jnp.where(i >= ctx_from, 1, 0), 0, 0)),
                  row(N_BRANCHES * d), row(BRANCH_WIDTH), row(BRANCH_WIDTH), row(BRANCH_WIDTH), row(BRANCH_WIDTH),
                  _resident((N_BRANCHES, BRANCH_WIDTH, d)), _resident((d, d))],
        out_specs=row(d),
        out_shape=jax.ShapeDtypeStruct((bsz, m_tot, d), F32),
        compiler_params=_cparams(2),
        name="merge",
    )(x3, g1, gates, *outs, w_br, w_out)


FFN_COL_CHUNK = 256


def _ffn_kernel(x_ref, xp_ref, xn_ref, sh_ref, sc_ref, g2_ref, gn_ref, wu_ref, cw_ref, cb_ref, wd_ref,
                out_ref, act_ref, *, n_tiles, ctx_from, d_ff):
    tm = x_ref.shape[1]
    i = pl.program_id(1)
    has_prev = jnp.logical_and(i != 0, i != ctx_from)
    has_next = jnp.logical_and(i != ctx_from - 1, i != n_tiles - 1)
    gain, shift, scale = gn_ref[...], sh_ref[0], sc_ref[0]
    xf = x_ref[0]
    h_prev = jnp.where(has_prev, _rms_mod(xp_ref[0], gain, shift, scale), 0.0)
    h_next = jnp.where(has_next, _rms_mod(xn_ref[0], gain, shift, scale), 0.0)
    h = jnp.concatenate([h_prev, _rms_mod(xf, gain, shift, scale), h_next], axis=0).astype(BF16)
    rows = tm + 2 * F32_SUBLANES

    def conv(c0):
        u = _dot(h, wu_ref[:, c0:c0 + FFN_COL_CHUNK])
        w = cw_ref[:, c0:c0 + FFN_COL_CHUNK]
        y = (w[0:1] * pltpu.roll(u, 1, axis=0) + w[1:2] * u + w[2:3] * pltpu.roll(u, rows - 1, axis=0)
             + cb_ref[:, c0:c0 + FFN_COL_CHUNK])
        return y[F32_SUBLANES:F32_SUBLANES + tm]

    for c in range(d_ff // FFN_COL_CHUNK):
        c0 = c * FFN_COL_CHUNK
        act_ref[:, c0:c0 + FFN_COL_CHUNK] = (_silu(conv(c0)) * conv(d_ff + c0)).astype(BF16)
    out_ref[0] = xf + g2_ref[0] * _dot(act_ref[...], wd_ref[...])


def _ffn(x3, mods, gain, w_up, conv_w, conv_b, w_down, tm, n_tiles, ctx_from, out_rows):
    bsz, m_tot, d = x3.shape
    d_ff = w_down.shape[0]
    hb = tm // F32_SUBLANES
    n_halo = m_tot // F32_SUBLANES
    mod_spec = pl.BlockSpec((1, 1, d), lambda b, i: (2 * b + jnp.where(i >= ctx_from, 1, 0), 0, 0))
    kern = functools.partial(_ffn_kernel, n_tiles=n_tiles, ctx_from=ctx_from, d_ff=d_ff)
    return pl.pallas_call(
        kern,
        grid=(bsz, n_tiles),
        in_specs=[pl.BlockSpec((1, tm, d), lambda b, i: (b, i, 0)),
                  pl.BlockSpec((1, F32_SUBLANES, d), lambda b, i: (b, jnp.maximum(i * hb - 1, 0), 0)),
                  pl.BlockSpec((1, F32_SUBLANES, d), lambda b, i: (b, jnp.minimum((i + 1) * hb, n_halo - 1), 0)),
                  mod_spec, mod_spec, mod_spec, _resident((1, d)), _resident((d, 2 * d_ff)),
                  _resident((3, 2 * d_ff)), _resident((1, 2 * d_ff)), _resident((d_ff, d))],
        out_specs=pl.BlockSpec((1, tm, d), lambda b, i: (b, i, 0)),
        out_shape=jax.ShapeDtypeStruct((bsz, out_rows, d), F32),
        scratch_shapes=[pltpu.VMEM((tm, d_ff), BF16)],
        compiler_params=_cparams(2),
        name="conv_ffn",
    )(x3, x3, x3, mods[3], mods[4], mods[5], gain.reshape(1, d), w_up, conv_w.astype(F32),
      conv_b.astype(F32).reshape(1, -1), w_down)


def _rope_tables(n_lat, m_ctx):
    t = jnp.arange(n_lat)
    row = (t // GRID_W).astype(F32)
    col = (t % GRID_W).astype(F32)
    n_freq = HEAD_DIM // 4
    inv = ROPE_BASE ** (-jnp.arange(n_freq, dtype=F32) / n_freq)
    ang = jnp.concatenate([row[:, None] * inv, col[:, None] * inv], axis=-1)
    cos, sin = jnp.cos(ang), jnp.sin(ang)
    reps = LANES // HEAD_DIM
    cos_l = jnp.tile(jnp.concatenate([cos, cos], axis=-1), (1, reps))
    sin_l = jnp.tile(jnp.concatenate([-sin, sin], axis=-1), (1, reps))
    return (jnp.concatenate([cos_l, jnp.ones((m_ctx, LANES), F32)], axis=0),
            jnp.concatenate([sin_l, jnp.zeros((m_ctx, LANES), F32)], axis=0))


def kernel(x, c, ctx, c_ctx, w_ada, b_ada, norm1_g, norm2_g, w_in, ssd_conv_w, ssd_conv_b, ssd_dt_bias, ssd_a_log, ssd_d, ssd_norm_g, gqa_qnorm_g, gqa_knorm_g, diff_qnorm_g, diff_knorm_g, diff_lambda, diff_subln_g, rg_conv_w, rg_conv_b, rg_wa, rg_ba, rg_wx, rg_bx, rg_lambda, w_gate, b_gate, w_br, w_out, w_up, ffn_conv_w, ffn_conv_b, w_down):
    bsz, n_lat, d = x.shape
    m_ctx = ctx.shape[1]
    m_tot = n_lat + m_ctx
    depth = w_in.shape[0]
    tm = math.gcd(math.gcd(m_ctx, n_lat), TOKEN_TILE)
    tpb, lat_tiles = m_tot // tm, n_lat // tm
    tm_lat = _row_tile(n_lat, WIDE_TILE)
    tk = min(KEY_CHUNK, m_tot)
    segs = ((n_lat, m_ctx), (0, n_lat))
    cos_all, sin_all = _rope_tables(n_lat, m_ctx)
    qscale = HEAD_DIM ** -0.5 * math.log2(math.e)
    seg = lax.broadcasted_iota(jnp.int32, (GQA_WIDTH, GQA_WIDTH), 0) // HEAD_DIM
    ones_bd = (seg == seg.T).astype(BF16)

    x_all = jnp.concatenate([x, ctx], axis=1)
    n_c = bsz + 1
    n_c_pad = -(-n_c // F32_SUBLANES) * F32_SUBLANES
    cc = jnp.pad(jnp.concatenate([c, c_ctx[None, :]], axis=0), ((0, n_c_pad - n_c), (0, 0)))
    dt_col = IN_SPLITS[0] + IN_SPLITS[1]

    specs = ((512, "plain", False, 1.0), (1024, "plain", False, 1.0), (DT_PAD, "plain", False, 1.0),
             (512, "qk", True, qscale), (128, "qk", False, 1.0), (128, "vt", True, 1.0),
             (512, "qk", True, qscale), (512, "qk", False, 1.0), (512, "vt", True, 1.0),
             (512, "plain", False, 1.0), (512, "plain", False, 1.0), (N_BRANCHES * d, "sigmoid", False, 1.0))
    dtypes = (BF16, BF16, F32) + (BF16,) * 9
    in_width = sum(s[0] for s in specs[:-1])

    for l in range(depth):
        last = l == depth - 1
        lambda_init = 0.8 - 0.6 * math.exp(-0.3 * l)

        mod = _ada(cc, w_ada[l], b_ada[l])
        mod_pairs = jnp.stack([mod[:bsz], jnp.broadcast_to(mod[bsz:bsz + 1], (bsz, 6 * d))], axis=1)
        mods = [mod_pairs[:, :, k * d:(k + 1) * d].reshape(2 * bsz, 1, d) for k in range(6)]

        wi = w_in[l]
        w_cat = jnp.concatenate(
            [wi[:, :dt_col + 16], jnp.zeros((d, DT_PAD - 16), wi.dtype), wi[:, dt_col + 16:],
             jnp.transpose(w_gate[l], (1, 0, 2)).reshape(d, N_BRANCHES * d)], axis=1).astype(BF16)
        b_cat = jnp.concatenate([jnp.zeros((in_width,), F32), b_gate[l].reshape(-1).astype(F32)]).reshape(1, -1)
        tile_g = lambda g, w: jnp.tile(g.astype(F32), w // HEAD_DIM)
        ones = lambda w: jnp.ones((w,), F32)
        post_gain = jnp.concatenate(
            [ones(512 + 1024 + DT_PAD), tile_g(gqa_qnorm_g[l], 512), tile_g(gqa_knorm_g[l], 128), ones(128),
             tile_g(diff_qnorm_g[l], 512), tile_g(diff_knorm_g[l], 512), ones(512 + 512 + 512 + N_BRANCHES * d)]
        ).reshape(1, -1)

        (z, xbc, dt_raw, gqt, gk, gvt, dqt, dk, dvt, rgg, rgx, gates) = _in_proj(
            x_all, mods, norm1_g[l], w_cat, b_cat, post_gain, ones_bd, cos_all, sin_all, specs, dtypes, tm, lat_tiles)

        ssd_o = _ssd(xbc, dt_raw, z, ssd_conv_w[l], ssd_conv_b[l], ssd_dt_bias[l], ssd_a_log[l], ssd_d[l],
                     ssd_norm_g[l], segs)
        rg_o = _rglru(rgx, rgg, rg_conv_w[l], rg_conv_b[l], rg_wa[l], rg_ba[l], rg_wx[l], rg_bx[l], rg_lambda[l],
                      segs)
        lat_q = dict(q_off=0, n_q=n_lat, key_off=0, n_keys=m_tot, tk=tk)
        ctx_q = dict(q_off=n_lat, n_q=m_ctx, key_off=n_lat, n_keys=m_ctx, tk=tk)
        gqa_o = _gqa_attention(gqt, gk, gvt, **lat_q)
        diff_o = _diff_attention(dqt, dk, dvt, diff_lambda[l], diff_subln_g[l], lambda_init, **lat_q)
        if not last:
            gqa_o = _gqa_attention(gqt, gk, gvt, prev_out=gqa_o, **ctx_q)
            diff_o = _diff_attention(dqt, dk, dvt, diff_lambda[l], diff_subln_g[l], lambda_init, prev_out=diff_o,
                                     **ctx_q)

        t_rows, t_tiles, t_ctx = (tm_lat, n_lat // tm_lat, n_lat // tm_lat) if last else (tm, tpb, lat_tiles)
        x_all = _merge(x_all, mods[2], gates, (ssd_o, gqa_o, diff_o, rg_o), w_br[l].astype(BF16),
                       w_out[l].astype(BF16), t_rows, t_tiles, t_ctx)
        x_all = _ffn(x_all, mods, norm2_g[l], w_up[l].astype(BF16), ffn_conv_w[l], ffn_conv_b[l],
                     w_down[l].astype(BF16), t_rows, t_tiles, t_ctx, n_lat if last else m_tot)
    return x_all
```

```python
import functools
import math

import jax
import jax.numpy as jnp
from jax import lax
from jax.experimental import pallas as pl
from jax.experimental.pallas import tpu as pltpu

F32 = jnp.float32
BF16 = jnp.bfloat16

GRID_W = 64
HEAD_DIM = 64
ROPE_BASE = 10000.0
NORM_EPS = 1e-6
SSD_HEADS = 8
SSD_HEAD_DIM = 64
SSD_INNER = 512
SSD_GROUPS = 2
SSD_STATE = 128
SSD_XBC = 1024
GQA_WIDTH = 512
GQA_KV_WIDTH = 128
DIFF_HEADS = 4
DIFF_WIDTH = 512
RG_WIDTH = 512
RG_C = 8.0
BRANCH_WIDTH = 512
N_BRANCHES = 4
IN_SPLITS = (512, 1024, 16, 512, 128, 128, 512, 512, 512, 512, 512)
DT_PAD = 128

VMEM_LIMIT_BYTES = 56 * 1024 * 1024
LANES = 128
F32_SUBLANES = 8
ROW_HALO = 16
CHUNK = 128
TOKEN_TILE = 256
WIDE_TILE = 512


def _cparams(n):
    return pltpu.CompilerParams(dimension_semantics=("parallel",) * n, vmem_limit_bytes=VMEM_LIMIT_BYTES)


def _resident(shape):
    nd = len(shape)
    return pl.BlockSpec(shape, lambda *_: (0,) * nd, pipeline_mode=pl.Buffered(1))


def _sigmoid(x):
    return 0.5 * jnp.tanh(0.5 * x) + 0.5


def _silu(x):
    h = 0.5 * x
    return h + h * jnp.tanh(h)


def _dot(a, b):
    return jnp.dot(a, b, preferred_element_type=F32)


def _row_tile(n, want):
    t = min(want, n)
    while n % t:
        t //= 2
    return t


def _rms_mod(xf, gain, shift, scale):
    ms = jnp.mean(xf * xf, axis=-1, keepdims=True)
    return xf * lax.rsqrt(ms + NORM_EPS) * gain * (1.0 + scale) + shift


def _ada_kernel(c_ref, w_ref, b_ref, o_ref):
    h = _silu(c_ref[...]).astype(BF16)
    o_ref[...] = _dot(h, w_ref[...].astype(BF16)) + b_ref[...]


def _ada(cc, w, b):
    r, d = cc.shape
    n = w.shape[1]
    tn = _row_tile(n, 1536)
    return pl.pallas_call(
        _ada_kernel,
        grid=(n // tn,),
        in_specs=[pl.BlockSpec((r, d), lambda j: (0, 0)),
                  pl.BlockSpec((d, tn), lambda j: (0, j)),
                  pl.BlockSpec((1, tn), lambda j: (0, j))],
        out_specs=pl.BlockSpec((r, tn), lambda j: (0, j)),
        out_shape=jax.ShapeDtypeStruct((r, n), F32),
        compiler_params=_cparams(1),
        name="ada",
    )(cc, w, b.reshape(1, n))


def _proj_kernel(x_ref, sh_ref, sc_ref, g_ref, w_ref, b_ref, pg_ref, ones_ref, cos_ref, sin_ref, *out_refs, specs):
    h = _rms_mod(x_ref[0], g_ref[...], sh_ref[0], sc_ref[0]).astype(BF16)
    tm = h.shape[0]
    lane = lax.broadcasted_iota(jnp.int32, (tm, LANES), 1)
    first_half = (lane % HEAD_DIM) < (HEAD_DIM // 2)
    c0 = 0
    for o_ref, (w, kind, transpose, scale) in zip(out_refs, specs):
        if kind in ("plain", "sigmoid") and not transpose:
            for s in range(0, w, 512):
                e = min(s + 512, w)
                acc = _dot(h, w_ref[:, c0 + s:c0 + e])
                if kind == "sigmoid":
                    acc = _sigmoid(acc + b_ref[:, c0 + s:c0 + e])
                o_ref[0, :, s:e] = acc.astype(o_ref.dtype)
        else:
            y = _dot(h, w_ref[:, c0:c0 + w])
            if kind == "qk":
                ss = _dot((y * y).astype(BF16), ones_ref[0:w, 0:w])
                y = y * lax.rsqrt(ss * (1.0 / HEAD_DIM) + NORM_EPS) * pg_ref[:, c0:c0 + w]
                cos = cos_ref[...]
                sin = sin_ref[...]
                cols = []
                for j in range(w // LANES):
                    yj = y[:, j * LANES:(j + 1) * LANES]
                    partner = jnp.where(first_half, pltpu.roll(yj, LANES - HEAD_DIM // 2, axis=1),
                                        pltpu.roll(yj, HEAD_DIM // 2, axis=1))
                    cols.append(yj * cos + partner * sin)
                y = cols[0] if len(cols) == 1 else jnp.concatenate(cols, axis=1)
                if scale != 1.0:
                    y = y * scale
            if transpose:
                o_ref[0] = y.T.astype(o_ref.dtype)
            else:
                o_ref[0] = y.astype(o_ref.dtype)
        c0 += w


def _in_proj(x3, mods, gain, w_cat, b_cat, post_gain, ones_bd, cos_all, sin_all, specs, dtypes, tm, lat_tiles):
    bsz, m_tot, d = x3.shape
    tpb = m_tot // tm
    ncat = w_cat.shape[1]

    def mod_idx(i):
        return (2 * (i // tpb) + jnp.where(i % tpb >= lat_tiles, 1, 0), 0, 0)

    out_specs, out_shapes = [], []
    for (w, _, transpose, _), dt in zip(specs, dtypes):
        if transpose:
            out_specs.append(pl.BlockSpec((1, w, tm), lambda i: (i // tpb, 0, i % tpb)))
            out_shapes.append(jax.ShapeDtypeStruct((bsz, w, m_tot), dt))
        else:
            out_specs.append(pl.BlockSpec((1, tm, w), lambda i: (i // tpb, i % tpb, 0)))
            out_shapes.append(jax.ShapeDtypeStruct((bsz, m_tot, w), dt))
    kern = functools.partial(_proj_kernel, specs=tuple(specs))
    return pl.pallas_call(
        kern,
        grid=(bsz * tpb,),
        in_specs=[pl.BlockSpec((1, tm, d), lambda i: (i // tpb, i % tpb, 0)),
                  pl.BlockSpec((1, 1, d), mod_idx), pl.BlockSpec((1, 1, d), mod_idx),
                  _resident((1, d)), _resident((d, ncat)), _resident((1, ncat)), _resident((1, ncat)),
                  _resident(ones_bd.shape),
                  pl.BlockSpec((tm, LANES), lambda i: (i % tpb, 0)),
                  pl.BlockSpec((tm, LANES), lambda i: (i % tpb, 0))],
        out_specs=out_specs,
        out_shape=out_shapes,
        compiler_params=_cparams(1),
        name="in_proj",
    )(x3, mods[0], mods[1], gain.reshape(1, d), w_cat, b_cat, post_gain, ones_bd, cos_all, sin_all)


def _conv_window(src, off, n, r0, rows):
    cur = src[0, pl.ds(r0, rows), :].astype(F32)
    lo = pl.multiple_of(jnp.maximum(r0 - ROW_HALO, off), ROW_HALO)
    hi = pl.multiple_of(jnp.minimum(r0 + rows, off + n - ROW_HALO), ROW_HALO)
    prev = src[0, pl.ds(lo, ROW_HALO), :].astype(F32)
    nxt = src[0, pl.ds(hi, ROW_HALO), :].astype(F32)
    prev = jnp.where(r0 > off, prev, 0.0)
    nxt = jnp.where(r0 + rows < off + n, nxt, 0.0)
    return jnp.concatenate([prev, cur, nxt], axis=0)


def _dwconv4(win, rows, cw_ref, cb_ref):
    total = rows + 2 * ROW_HALO
    acc = cb_ref[...] + cw_ref[1:2, :] * win[ROW_HALO:ROW_HALO + rows]
    for k in (0, 2, 3):
        sh = pltpu.roll(win, (1 - k) % total, axis=0)[ROW_HALO:ROW_HALO + rows]
        acc = acc + cw_ref[k:k + 1, :] * sh
    return acc


def _cumsum_rows(a, reverse):
    n = a.shape[0]
    rid = lax.broadcasted_iota(jnp.int32, a.shape, 0)
    k = 1
    while k < n:
        if reverse:
            a = a + jnp.where(rid < n - k, pltpu.roll(a, n - k, axis=0), 0.0)
        else:
            a = a + jnp.where(rid >= k, pltpu.roll(a, k, axis=0), 0.0)
        k *= 2
    return a


def _ssd_kernel(xbc, dtr, z, cw_ref, cb_ref, dtb_ref, aneg_ref, dvec_ref, ng_ref, sel_ref,
                out, xc, ybuf, st_ref, *, segs):
    heads_per_group = SSD_HEADS // SSD_GROUPS
    gw = heads_per_group * SSD_HEAD_DIM
    pair_w = 2 * SSD_HEAD_DIM

    for off, n in segs:
        def conv_body(j, carry, off=off, n=n):
            r0 = pl.multiple_of(off + j * CHUNK, CHUNK)
            win = _conv_window(xbc, off, n, r0, CHUNK)
            xc[pl.ds(r0, CHUNK), :] = _silu(_dwconv4(win, CHUNK, cw_ref, cb_ref)).astype(BF16)
            return carry
        lax.fori_loop(0, n // CHUNK, conv_body, 0)

    rid = lax.broadcasted_iota(jnp.int32, (CHUNK, CHUNK), 0)
    cid = lax.broadcasted_iota(jnp.int32, (CHUNK, CHUNK), 1)
    low_half = cid < SSD_HEAD_DIM

    def chunk(r0, d):
        dt = jax.nn.softplus(dtr[0, pl.ds(r0, CHUNK), :] + dtb_ref[...])
        cs = _cumsum_rows(dt * aneg_ref[...], reverse=(d == 1))
        cst = cs.T
        tot = cs[CHUNK - 1:CHUNK, :] if d == 0 else cs[0:1, :]
        mask = (rid >= cid) if d == 0 else (cid >= rid)
        etot = jnp.exp(tot)
        et_hi = etot.astype(BF16).astype(F32)
        rows8 = lambda v: jnp.broadcast_to(v, (F32_SUBLANES, v.shape[1]))
        stack = jnp.concatenate([dt, dt * jnp.exp(tot - cs), jnp.exp(cs), rows8(et_hi), rows8(etot - et_hi)],
                                axis=0).astype(BF16)
        spread = _dot(stack, sel_ref[d])
        dt_b = spread[0:CHUNK]
        w_b = spread[CHUNK:2 * CHUNK]
        e_b = spread[2 * CHUNK:3 * CHUNK]
        t_b = spread[3 * CHUNK:3 * CHUNK + 1] + spread[3 * CHUNK + F32_SUBLANES:3 * CHUNK + F32_SUBLANES + 1]
        xs = xc[pl.ds(r0, CHUNK), 0:SSD_INNER].astype(F32)
        xdt = xs * dt_b
        xdec = (xs * w_b).astype(BF16)
        ys = []
        for g in range(SSD_GROUPS):
            bg = xc[pl.ds(r0, CHUNK), SSD_INNER + g * SSD_STATE:SSD_INNER + (g + 1) * SSD_STATE]
            cg = xc[pl.ds(r0, CHUNK), SSD_INNER + (SSD_GROUPS + g) * SSD_STATE:
                    SSD_INNER + (SSD_GROUPS + g + 1) * SSD_STATE]
            bgt = bg.astype(F32).T.astype(BF16)
            gmat = _dot(cg, bgt)
            ypairs = []
            for q in range(heads_per_group // 2):
                scs = []
                for hl in (2 * q, 2 * q + 1):
                    col = d * SSD_HEADS + g * heads_per_group + hl
                    diff = cs[:, col:col + 1] - cst[col:col + 1, :]
                    scs.append((gmat * jnp.exp(jnp.where(mask, diff, -jnp.inf))).astype(BF16))
                c0 = g * gw + q * pair_w
                xp = xdt[:, c0:c0 + pair_w]
                rhs = jnp.concatenate([jnp.where(low_half, xp, 0.0).astype(BF16),
                                       jnp.where(low_half, 0.0, xp).astype(BF16)], axis=0)
                ypairs.append(_dot(jnp.concatenate(scs, axis=1), rhs))
            st = st_ref[d * SSD_GROUPS + g]
            yoff = _dot(cg, st.astype(BF16))
            ys.append(jnp.concatenate(ypairs, axis=1) + yoff * e_b[:, g * gw:(g + 1) * gw])
            st_ref[d * SSD_GROUPS + g] = st * t_b[:, g * gw:(g + 1) * gw] + _dot(bgt, xdec[:, g * gw:(g + 1) * gw])
        return ys

    def emit(r0, y_groups):
        zz = _silu(z[0, pl.ds(r0, CHUNK), :].astype(F32))
        gsz = SSD_INNER // SSD_GROUPS
        for g in range(SSD_GROUPS):
            cols = slice(g * gsz, (g + 1) * gsz)
            xs = xc[pl.ds(r0, CHUNK), cols].astype(F32)
            yg = (y_groups[g] + xs * dvec_ref[:, cols]) * zz[:, cols]
            ms = jnp.mean(yg * yg, axis=-1, keepdims=True)
            out[0, pl.ds(r0, CHUNK), cols] = (yg * lax.rsqrt(ms + NORM_EPS) * ng_ref[:, cols]).astype(out.dtype)

    def stash(r0, y_groups):
        for g in range(SSD_GROUPS):
            ybuf[pl.ds(r0, CHUNK), g * gw:(g + 1) * gw] = y_groups[g]

    def with_stash(r0, y_groups):
        return [y_groups[g] + ybuf[pl.ds(r0, CHUNK), g * gw:(g + 1) * gw] for g in range(SSD_GROUPS)]

    st_ref[...] = jnp.zeros_like(st_ref)
    for off, n in segs:
        nch = n // CHUNK

        def rows(k, off=off, nch=nch):
            return (pl.multiple_of(off + k * CHUNK, CHUNK), pl.multiple_of(off + (nch - 1 - k) * CHUNK, CHUNK))

        def first_half(k, carry, rows=rows):
            rf, rb = rows(k)
            stash(rf, chunk(rf, 0))
            stash(rb, chunk(rb, 1))
            return carry
        lax.fori_loop(0, nch // 2, first_half, 0)

        if nch % 2:
            r_mid = off + (nch // 2) * CHUNK
            yf_mid, yb_mid = chunk(r_mid, 0), chunk(r_mid, 1)
            emit(r_mid, [yf_mid[g] + yb_mid[g] for g in range(SSD_GROUPS)])

        def second_half(k, carry, rows=rows):
            rf, rb = rows(k)
            emit(rf, with_stash(rf, chunk(rf, 0)))
            emit(rb, with_stash(rb, chunk(rb, 1)))
            return carry
        lax.fori_loop((nch + 1) // 2, nch, second_half, 0)


def _ssd(xbc, dt_raw, z, conv_w, conv_b, dt_bias, a_log, d_skip, norm_g, segs):
    bsz, m_tot, _ = xbc.shape
    pad = DT_PAD - 2 * SSD_HEADS
    dtb = jnp.pad(dt_bias.astype(F32).reshape(1, -1), ((0, 0), (0, pad)))
    aneg = jnp.pad(-jnp.exp(a_log.astype(F32)).reshape(1, -1), ((0, 0), (0, pad)))
    dvec = jnp.repeat(d_skip.astype(F32), SSD_HEAD_DIM).reshape(1, SSD_INNER)
    col = jnp.arange(DT_PAD)[None, :, None]
    head = (jnp.arange(SSD_INNER) // SSD_HEAD_DIM)[None, None, :]
    sel = (col == jnp.arange(2)[:, None, None] * SSD_HEADS + head).astype(BF16)

    def seq_spec(w, single):
        kw = dict(pipeline_mode=pl.Buffered(1)) if single else {}
        return pl.BlockSpec((1, m_tot, w), lambda b: (b, 0, 0), **kw)

    kern = functools.partial(_ssd_kernel, segs=segs)
    return pl.pallas_call(
        kern,
        grid=(bsz,),
        in_specs=[seq_spec(SSD_XBC, True), seq_spec(DT_PAD, True), seq_spec(SSD_INNER, True),
                  _resident((4, SSD_XBC)), _resident((1, SSD_XBC)), _resident((1, DT_PAD)), _resident((1, DT_PAD)),
                  _resident((1, SSD_INNER)), _resident((1, SSD_INNER)), _resident((2, DT_PAD, SSD_INNER))],
        out_specs=seq_spec(SSD_INNER, False),
        out_shape=jax.ShapeDtypeStruct((bsz, m_tot, SSD_INNER), BF16),
        scratch_shapes=[pltpu.VMEM((m_tot, SSD_XBC), BF16), pltpu.VMEM((m_tot, SSD_INNER), F32),
                        pltpu.VMEM((2 * SSD_GROUPS, SSD_STATE, SSD_INNER // SSD_GROUPS), F32)],
        compiler_params=_cparams(1),
        name="ssd",
    )(xbc, dt_raw, z, conv_w.astype(F32), conv_b.astype(F32).reshape(1, -1), dtb, aneg, dvec,
      norm_g.astype(F32).reshape(1, -1), sel)


def _gelu_tanh(x):
    return 0.5 * x * (1.0 + jnp.tanh(math.sqrt(2.0 / math.pi) * (x + 0.044715 * (x * x * x))))


def _rg_kernel(x, gg, cw_ref, cb_ref, w_ref, b_ref, lam_ref, out, hbuf, a_s, b_s, h_s, hcar, *, segs):
    gw = 2 * RG_WIDTH

    def gates(off, n, r0, d):
        win = _conv_window(x, off, n, r0, CHUNK)
        xr = _dwconv4(win, CHUNK, cw_ref, cb_ref)
        g = _dot(xr.astype(BF16), w_ref[:, d * gw:(d + 1) * gw]) + b_ref[:, d * gw:(d + 1) * gw]
        sp = jax.nn.softplus(-lam_ref[d:d + 1, :])
        log_a = -RG_C * _sigmoid(g[:, :RG_WIDTH]) * sp
        a = jnp.exp(log_a)
        a_s[d] = a
        one_minus_a2 = -jnp.tanh(log_a) * (a * a + 1.0)
        b_s[d] = jnp.sqrt(one_minus_a2) * (_sigmoid(g[:, RG_WIDTH:]) * xr)

    def scan_both():
        blk = 16

        def block(i, carry):
            hf, hb = carry
            f0 = pl.multiple_of(i * blk, blk)
            b0 = pl.multiple_of(CHUNK - blk - i * blk, blk)
            for k in range(blk):
                kb = blk - 1 - k
                hf = a_s[0, pl.ds(f0 + k, 1), :] * hf + b_s[0, pl.ds(f0 + k, 1), :]
                hb = a_s[1, pl.ds(b0 + kb, 1), :] * hb + b_s[1, pl.ds(b0 + kb, 1), :]
                h_s[0, pl.ds(f0 + k, 1), :] = hf
                h_s[1, pl.ds(b0 + kb, 1), :] = hb
            return hf, hb
        hf, hb = lax.fori_loop(0, CHUNK // blk, block, (hcar[0:1, :], hcar[1:2, :]))
        hcar[0:1, :] = hf
        hcar[1:2, :] = hb

    def emit(r0, hsum):
        gate = _gelu_tanh(gg[0, pl.ds(r0, CHUNK), :].astype(F32))
        out[0, pl.ds(r0, CHUNK), :] = (hsum * gate).astype(out.dtype)

    hcar[...] = jnp.zeros_like(hcar)
    for off, n in segs:
        nch = n // CHUNK

        def rows(k, off=off, nch=nch):
            return (pl.multiple_of(off + k * CHUNK, CHUNK), pl.multiple_of(off + (nch - 1 - k) * CHUNK, CHUNK))

        def first_half(k, carry, off=off, n=n, rows=rows):
            rf, rb = rows(k)
            gates(off, n, rf, 0)
            gates(off, n, rb, 1)
            scan_both()
            hbuf[pl.ds(rf, CHUNK), :] = h_s[0]
            hbuf[pl.ds(rb, CHUNK), :] = h_s[1]
            return carry
        lax.fori_loop(0, nch // 2, first_half, 0)

        if nch % 2:
            r_mid = off + (nch // 2) * CHUNK
            gates(off, n, r_mid, 0)
            gates(off, n, r_mid, 1)
            scan_both()
            emit(r_mid, h_s[0] + h_s[1])

        def second_half(k, carry, off=off, n=n, rows=rows):
            rf, rb = rows(k)
            gates(off, n, rf, 0)
            gates(off, n, rb, 1)
            scan_both()
            emit(rf, h_s[0] + hbuf[pl.ds(rf, CHUNK), :])
            emit(rb, hbuf[pl.ds(rb, CHUNK), :] + h_s[1])
            return carry
        lax.fori_loop((nch + 1) // 2, nch, second_half, 0)


def _block_diag(w):
    k, c, e = w.shape
    return jnp.einsum("kce,kj->kcje", w, jnp.eye(k, dtype=w.dtype)).reshape(k * c, k * e)


def _rglru(rgx, rgg, conv_w, conv_b, wa, ba, wx, bx, lam, segs):
    bsz, m_tot, _ = rgx.shape
    w_cat = jnp.concatenate([_block_diag(wa[0]), _block_diag(wx[0]), _block_diag(wa[1]), _block_diag(wx[1])],
                            axis=1).astype(BF16)
    b_cat = jnp.concatenate([ba[0], bx[0], ba[1], bx[1]]).astype(F32).reshape(1, -1)
    seq_spec = pl.BlockSpec((1, m_tot, RG_WIDTH), lambda b: (b, 0, 0))
    kern = functools.partial(_rg_kernel, segs=segs)
    return pl.pallas_call(
        kern,
        grid=(bsz,),
        in_specs=[seq_spec, seq_spec,
                  _resident((4, RG_WIDTH)), _resident((1, RG_WIDTH)), _resident((RG_WIDTH, 4 * RG_WIDTH)),
                  _resident((1, 4 * RG_WIDTH)), _resident((2, RG_WIDTH))],
        out_specs=seq_spec,
        out_shape=jax.ShapeDtypeStruct((bsz, m_tot, RG_WIDTH), BF16),
        scratch_shapes=[pltpu.VMEM((m_tot, RG_WIDTH), F32),
                        pltpu.VMEM((2, CHUNK, RG_WIDTH), F32), pltpu.VMEM((2, CHUNK, RG_WIDTH), F32),
                        pltpu.VMEM((2, CHUNK, RG_WIDTH), F32), pltpu.VMEM((2, RG_WIDTH), F32)],
        compiler_params=_cparams(1),
        name="rglru",
    )(rgx, rgg, conv_w.astype(F32), conv_b.astype(F32).reshape(1, -1), w_cat, b_cat, lam.astype(F32))


KEY_CHUNK = 256
ATTN_COLS = 1024
ONES_ROWS = 16


def _attn_core(qt_ext, k_ref, vt_ref, key_off, n_keys, tk):
    chunks = [(key_off + o, min(tk, n_keys - o)) for o in range(0, n_keys, tk)]
    n_chunks = len(chunks)

    def scores(j):
        off, size = chunks[min(j, n_chunks - 1)]
        s = _dot(k_ref[0, off:off + size, :], qt_ext)
        return s, jnp.max(s, axis=0, keepdims=True)

    def softmax(s, m_old, cmax):
        m_new = jnp.maximum(m_old, cmax)
        return jnp.exp2((s - m_new).astype(BF16)), m_new, jnp.exp2(m_old - m_new)

    def accumulate(j, acc, p, alpha):
        off, size = chunks[j]
        vt = jnp.concatenate([vt_ref[0, :, off:off + size], jnp.ones((ONES_ROWS, size), BF16)], axis=0)
        pv = _dot(vt, p)
        return pv if acc is None else alpha * acc + pv

    s_cur, c_cur = scores(0)
    s_nxt, c_nxt = scores(1)
    p_prev, m, a_prev = softmax(s_cur, jnp.full(c_cur.shape, -jnp.inf, F32), c_cur)
    acc = None
    for j in range(1, n_chunks):
        acc = accumulate(j - 1, acc, p_prev, a_prev)
        s_cur, c_cur = s_nxt, c_nxt
        s_nxt, c_nxt = scores(j + 1)
        p_prev, m, a_prev = softmax(s_cur, m, c_cur)
    return accumulate(n_chunks - 1, acc, p_prev, a_prev)


def _gqa_kernel(qt_ref, k_ref, vt_ref, *rest, key_off, n_keys, tk, tq, rep, kv_heads):
    o_ref = rest[-1]
    qt = qt_ref[0]
    qcat = jnp.concatenate([qt[r * HEAD_DIM:(r + 1) * HEAD_DIM, :] for r in range(rep)], axis=1)
    g = pl.program_id(1)
    zero = jnp.zeros_like(qcat)
    qt_ext = jnp.concatenate([jnp.where(g == j, qcat, zero) for j in range(kv_heads)], axis=0)
    acc = _attn_core(qt_ext, k_ref, vt_ref, key_off, n_keys, tk)
    o = acc[0:HEAD_DIM, :] / acc[HEAD_DIM:HEAD_DIM + 1, :]
    o2 = jnp.concatenate([o[:, r * tq:(r + 1) * tq] for r in range(rep)], axis=0)
    o_ref[0] = o2.T.astype(o_ref.dtype)


def _alias_args(prev_out, n_inputs):
    if prev_out is None:
        return [], [], {}
    return [pl.BlockSpec(memory_space=pl.ANY)], [prev_out], {n_inputs: 0}


def _gqa_attention(qt, k_all, vt_all, *, q_off, n_q, key_off, n_keys, tk, prev_out=None):
    bsz, _, m_tot = qt.shape
    kv_heads = GQA_KV_WIDTH // HEAD_DIM
    rep = GQA_WIDTH // GQA_KV_WIDTH
    tq = _row_tile(math.gcd(n_q, q_off) if q_off else n_q, ATTN_COLS // rep)
    qb0 = q_off // tq
    qw = rep * HEAD_DIM
    alias_specs, alias_args, aliases = _alias_args(prev_out, 3)
    kern = functools.partial(_gqa_kernel, key_off=key_off, n_keys=n_keys, tk=tk, tq=tq, rep=rep, kv_heads=kv_heads)
    return pl.pallas_call(
        kern, grid=(bsz, kv_heads, n_q // tq),
        in_specs=[pl.BlockSpec((1, qw, tq), lambda b, g, i: (b, g, qb0 + i)),
                  pl.BlockSpec((1, m_tot, GQA_KV_WIDTH), lambda b, g, i: (b, 0, 0)),
                  pl.BlockSpec((1, HEAD_DIM, m_tot), lambda b, g, i: (b, g, 0))] + alias_specs,
        out_specs=pl.BlockSpec((1, tq, qw), lambda b, g, i: (b, qb0 + i, g)),
        out_shape=jax.ShapeDtypeStruct((bsz, m_tot, GQA_WIDTH), BF16),
        input_output_aliases=aliases,
        compiler_params=_cparams(3), name="gqa_attn")(qt, k_all, vt_all, *alias_args)


def _diff_kernel(qt_ref, k_ref, vt_ref, lam_ref, sg_ref, *rest, key_off, n_keys, tk, tq, lambda_init):
    o_ref = rest[-1]
    qt = qt_ref[0]
    z = jnp.zeros((HEAD_DIM, tq), qt.dtype)
    qt_ext = jnp.concatenate([jnp.concatenate([qt[:HEAD_DIM], z], axis=1),
                              jnp.concatenate([z, qt[HEAD_DIM:]], axis=1)], axis=0)
    acc = _attn_core(qt_ext, k_ref, vt_ref, key_off, n_keys, tk)
    vw = 2 * HEAD_DIM
    o = acc[0:vw, :] / acc[vw:vw + 1, :]
    lp = lam_ref[...]
    lam = (jnp.exp(jnp.sum(lp[0:1] * lp[1:2], axis=-1, keepdims=True))
           - jnp.exp(jnp.sum(lp[2:3] * lp[3:4], axis=-1, keepdims=True)) + lambda_init)
    o = (o[:, :tq] - lam * o[:, tq:]).T
    ms = jnp.mean(o * o, axis=-1, keepdims=True)
    o = o * lax.rsqrt(ms + NORM_EPS) * sg_ref[...] * (1.0 - lambda_init)
    o_ref[0] = o.astype(o_ref.dtype)


def _diff_attention(qt, k_all, vt_all, lam_p, subln_g, lambda_init, *, q_off, n_q, key_off, n_keys, tk,
                    prev_out=None):
    bsz, _, m_tot = qt.shape
    tq = _row_tile(math.gcd(n_q, q_off) if q_off else n_q, ATTN_COLS // 2)
    qb0 = q_off // tq
    vw = 2 * HEAD_DIM
    alias_specs, alias_args, aliases = _alias_args(prev_out, 5)
    kern = functools.partial(_diff_kernel, key_off=key_off, n_keys=n_keys, tk=tk, tq=tq, lambda_init=lambda_init)
    return pl.pallas_call(
        kern, grid=(bsz, DIFF_HEADS, n_q // tq),
        in_specs=[pl.BlockSpec((1, vw, tq), lambda b, h, i: (b, h, qb0 + i)),
                  pl.BlockSpec((1, m_tot, vw), lambda b, h, i: (b, 0, h)),
                  pl.BlockSpec((1, vw, m_tot), lambda b, h, i: (b, h, 0)),
                  _resident((4, HEAD_DIM)), _resident((1, vw))] + alias_specs,
        out_specs=pl.BlockSpec((1, tq, vw), lambda b, h, i: (b, qb0 + i, h)),
        out_shape=jax.ShapeDtypeStruct((bsz, m_tot, DIFF_WIDTH), BF16),
        input_output_aliases=aliases,
        compiler_params=_cparams(3), name="diff_attn")(qt, k_all, vt_all, lam_p.astype(F32),
                                                       subln_g.astype(F32).reshape(1, vw), *alias_args)


def _merge_kernel(x_ref, g1_ref, gates_ref, o0, o1, o2, o3, wbr_ref, wout_ref, *rest):
    out_ref = rest[-1]
    d = x_ref.shape[2]
    m = None
    for k, o in enumerate((o0, o1, o2, o3)):
        t = gates_ref[0, :, k * d:(k + 1) * d].astype(F32) * _dot(o[0], wbr_ref[k])
        m = t if m is None else m + t
    y = _dot(m.astype(BF16), wout_ref[...])
    out_ref[0] = x_ref[0] + g1_ref[0] * y


def _seg_tiles(seg):
    off, rows, _ = seg
    tm = _row_tile(math.gcd(rows, off) if off else rows, WIDE_TILE)
    return tm, off // tm, rows // tm


def _merge(x3, g1, gates, outs, w_br, w_out, seg, prev_out=None):
    bsz, m_tot, d = x3.shape
    tm, t0, n_tiles = _seg_tiles(seg)
    is_ctx = int(seg[2])
    row = lambda w: pl.BlockSpec((1, tm, w), lambda b, i: (b, t0 + i, 0))
    alias_specs, alias_args, aliases = _alias_args(prev_out, 9)
    return pl.pallas_call(
        _merge_kernel,
        grid=(bsz, n_tiles),
        in_specs=[row(d), pl.BlockSpec((1, 1, d), lambda b, i: (2 * b + is_ctx, 0, 0)),
                  row(N_BRANCHES * d), row(BRANCH_WIDTH), row(BRANCH_WIDTH), row(BRANCH_WIDTH), row(BRANCH_WIDTH),
                  _resident((N_BRANCHES, BRANCH_WIDTH, d)), _resident((d, d))] + alias_specs,
        out_specs=row(d),
        out_shape=jax.ShapeDtypeStruct((bsz, m_tot, d), F32),
        input_output_aliases=aliases,
        compiler_params=_cparams(2),
        name="merge",
    )(x3, g1, gates, *outs, w_br, w_out, *alias_args)


FFN_COL_CHUNK = 256


def _ffn_kernel(x_ref, xp_ref, xn_ref, sh_ref, sc_ref, g2_ref, gn_ref, wu_ref, cw_ref, cb_ref, wd_ref,
                *rest, n_tiles, d_ff):
    out_ref, act_ref = rest[-2:]
    tm = x_ref.shape[1]
    i = pl.program_id(1)
    has_prev = i != 0
    has_next = i != n_tiles - 1
    gain, shift, scale = gn_ref[...], sh_ref[0], sc_ref[0]
    xf = x_ref[0]
    h_prev = jnp.where(has_prev, _rms_mod(xp_ref[0], gain, shift, scale), 0.0)
    h_next = jnp.where(has_next, _rms_mod(xn_ref[0], gain, shift, scale), 0.0)
    h = jnp.concatenate([h_prev, _rms_mod(xf, gain, shift, scale), h_next], axis=0).astype(BF16)
    rows = tm + 2 * F32_SUBLANES

    def conv(c0):
        u = _dot(h, wu_ref[:, c0:c0 + FFN_COL_CHUNK])
        w = cw_ref[:, c0:c0 + FFN_COL_CHUNK]
        y = (w[0:1] * pltpu.roll(u, 1, axis=0) + w[1:2] * u + w[2:3] * pltpu.roll(u, rows - 1, axis=0)
             + cb_ref[:, c0:c0 + FFN_COL_CHUNK])
        return y[F32_SUBLANES:F32_SUBLANES + tm]

    for c in range(d_ff // FFN_COL_CHUNK):
        c0 = c * FFN_COL_CHUNK
        act_ref[:, c0:c0 + FFN_COL_CHUNK] = (_silu(conv(c0)) * conv(d_ff + c0)).astype(BF16)
    out_ref[0] = xf + g2_ref[0] * _dot(act_ref[...], wd_ref[...])


def _ffn(x3, mods, gain, w_up, conv_w, conv_b, w_down, seg, out_rows, prev_out=None):
    bsz, m_tot, d = x3.shape
    d_ff = w_down.shape[0]
    tm, t0, n_tiles = _seg_tiles(seg)
    is_ctx = int(seg[2])
    hb = tm // F32_SUBLANES
    n_halo = m_tot // F32_SUBLANES
    mod_spec = pl.BlockSpec((1, 1, d), lambda b, i: (2 * b + is_ctx, 0, 0))
    alias_specs, alias_args, aliases = _alias_args(prev_out, 11)
    kern = functools.partial(_ffn_kernel, n_tiles=n_tiles, d_ff=d_ff)
    return pl.pallas_call(
        kern,
        grid=(bsz, n_tiles),
        in_specs=[pl.BlockSpec((1, tm, d), lambda b, i: (b, t0 + i, 0)),
                  pl.BlockSpec((1, F32_SUBLANES, d), lambda b, i: (b, jnp.maximum((t0 + i) * hb - 1, 0), 0)),
                  pl.BlockSpec((1, F32_SUBLANES, d),
                               lambda b, i: (b, jnp.minimum((t0 + i + 1) * hb, n_halo - 1), 0)),
                  mod_spec, mod_spec, mod_spec, _resident((1, d)), _resident((d, 2 * d_ff)),
                  _resident((3, 2 * d_ff)), _resident((1, 2 * d_ff)), _resident((d_ff, d))] + alias_specs,
        out_specs=pl.BlockSpec((1, tm, d), lambda b, i: (b, t0 + i, 0)),
        out_shape=jax.ShapeDtypeStruct((bsz, out_rows, d), F32),
        scratch_shapes=[pltpu.VMEM((tm, d_ff), BF16)],
        input_output_aliases=aliases,
        compiler_params=_cparams(2),
        name="conv_ffn",
    )(x3, x3, x3, mods[3], mods[4], mods[5], gain.reshape(1, d), w_up, conv_w.astype(F32),
      conv_b.astype(F32).reshape(1, -1), w_down, *alias_args)


def _rope_tables(n_lat, m_ctx):
    t = jnp.arange(n_lat)
    row = (t // GRID_W).astype(F32)
    col = (t % GRID_W).astype(F32)
    n_freq = HEAD_DIM // 4
    inv = ROPE_BASE ** (-jnp.arange(n_freq, dtype=F32) / n_freq)
    ang = jnp.concatenate([row[:, None] * inv, col[:, None] * inv], axis=-1)
    cos, sin = jnp.cos(ang), jnp.sin(ang)
    reps = LANES // HEAD_DIM
    cos_l = jnp.tile(jnp.concatenate([cos, cos], axis=-1), (1, reps))
    sin_l = jnp.tile(jnp.concatenate([-sin, sin], axis=-1), (1, reps))
    return (jnp.concatenate([cos_l, jnp.ones((m_ctx, LANES), F32)], axis=0),
            jnp.concatenate([sin_l, jnp.zeros((m_ctx, LANES), F32)], axis=0))


def kernel(x, c, ctx, c_ctx, w_ada, b_ada, norm1_g, norm2_g, w_in, ssd_conv_w, ssd_conv_b, ssd_dt_bias, ssd_a_log, ssd_d, ssd_norm_g, gqa_qnorm_g, gqa_knorm_g, diff_qnorm_g, diff_knorm_g, diff_lambda, diff_subln_g, rg_conv_w, rg_conv_b, rg_wa, rg_ba, rg_wx, rg_bx, rg_lambda, w_gate, b_gate, w_br, w_out, w_up, ffn_conv_w, ffn_conv_b, w_down):
    bsz, n_lat, d = x.shape
    m_ctx = ctx.shape[1]
    m_tot = n_lat + m_ctx
    depth = w_in.shape[0]
    tm = math.gcd(math.gcd(m_ctx, n_lat), TOKEN_TILE)
    tpb, lat_tiles = m_tot // tm, n_lat // tm
    tk = min(KEY_CHUNK, m_tot)
    segs = ((n_lat, m_ctx), (0, n_lat))
    cos_all, sin_all = _rope_tables(n_lat, m_ctx)
    qscale = HEAD_DIM ** -0.5 * math.log2(math.e)
    seg = lax.broadcasted_iota(jnp.int32, (GQA_WIDTH, GQA_WIDTH), 0) // HEAD_DIM
    ones_bd = (seg == seg.T).astype(BF16)

    x_all = jnp.concatenate([x, ctx], axis=1)
    n_c = bsz + 1
    n_c_pad = -(-n_c // F32_SUBLANES) * F32_SUBLANES
    cc = jnp.pad(jnp.concatenate([c, c_ctx[None, :]], axis=0), ((0, n_c_pad - n_c), (0, 0)))
    dt_col = IN_SPLITS[0] + IN_SPLITS[1]

    specs = ((512, "plain", False, 1.0), (1024, "plain", False, 1.0), (DT_PAD, "plain", False, 1.0),
             (512, "qk", True, qscale), (128, "qk", False, 1.0), (128, "vt", True, 1.0),
             (512, "qk", True, qscale), (512, "qk", False, 1.0), (512, "vt", True, 1.0),
             (512, "plain", False, 1.0), (512, "plain", False, 1.0), (N_BRANCHES * d, "sigmoid", False, 1.0))
    dtypes = (BF16, BF16, F32) + (BF16,) * 9
    in_width = sum(s[0] for s in specs[:-1])

    for l in range(depth):
        last = l == depth - 1
        lambda_init = 0.8 - 0.6 * math.exp(-0.3 * l)

        mod = _ada(cc, w_ada[l], b_ada[l])
        mod_pairs = jnp.stack([mod[:bsz], jnp.broadcast_to(mod[bsz:bsz + 1], (bsz, 6 * d))], axis=1)
        mods = [mod_pairs[:, :, k * d:(k + 1) * d].reshape(2 * bsz, 1, d) for k in range(6)]

        wi = w_in[l]
        w_cat = jnp.concatenate(
            [wi[:, :dt_col + 16], jnp.zeros((d, DT_PAD - 16), wi.dtype), wi[:, dt_col + 16:],
             jnp.transpose(w_gate[l], (1, 0, 2)).reshape(d, N_BRANCHES * d)], axis=1).astype(BF16)
        b_cat = jnp.concatenate([jnp.zeros((in_width,), F32), b_gate[l].reshape(-1).astype(F32)]).reshape(1, -1)
        tile_g = lambda g, w: jnp.tile(g.astype(F32), w // HEAD_DIM)
        ones = lambda w: jnp.ones((w,), F32)
        post_gain = jnp.concatenate(
            [ones(512 + 1024 + DT_PAD), tile_g(gqa_qnorm_g[l], 512), tile_g(gqa_knorm_g[l], 128), ones(128),
             tile_g(diff_qnorm_g[l], 512), tile_g(diff_knorm_g[l], 512), ones(512 + 512 + 512 + N_BRANCHES * d)]
        ).reshape(1, -1)

        (z, xbc, dt_raw, gqt, gk, gvt, dqt, dk, dvt, rgg, rgx, gates) = _in_proj(
            x_all, mods, norm1_g[l], w_cat, b_cat, post_gain, ones_bd, cos_all, sin_all, specs, dtypes, tm, lat_tiles)

        ssd_o = _ssd(xbc, dt_raw, z, ssd_conv_w[l], ssd_conv_b[l], ssd_dt_bias[l], ssd_a_log[l], ssd_d[l],
                     ssd_norm_g[l], segs)
        rg_o = _rglru(rgx, rgg, rg_conv_w[l], rg_conv_b[l], rg_wa[l], rg_ba[l], rg_wx[l], rg_bx[l], rg_lambda[l],
                      segs)
        lat_q = dict(q_off=0, n_q=n_lat, key_off=0, n_keys=m_tot, tk=tk)
        ctx_q = dict(q_off=n_lat, n_q=m_ctx, key_off=n_lat, n_keys=m_ctx, tk=tk)
        gqa_o = _gqa_attention(gqt, gk, gvt, **lat_q)
        diff_o = _diff_attention(dqt, dk, dvt, diff_lambda[l], diff_subln_g[l], lambda_init, **lat_q)
        if not last:
            gqa_o = _gqa_attention(gqt, gk, gvt, prev_out=gqa_o, **ctx_q)
            diff_o = _diff_attention(dqt, dk, dvt, diff_lambda[l], diff_subln_g[l], lambda_init, prev_out=diff_o,
                                     **ctx_q)

        branch_outs = (ssd_o, gqa_o, diff_o, rg_o)
        wbr, wout = w_br[l].astype(BF16), w_out[l].astype(BF16)
        wup, wdn = w_up[l].astype(BF16), w_down[l].astype(BF16)
        lat_seg, ctx_seg = (0, n_lat, False), (n_lat, m_ctx, True)
        x_mid = _merge(x_all, mods[2], gates, branch_outs, wbr, wout, lat_seg)
        if last:
            return _ffn(x_mid, mods, norm2_g[l], wup, ffn_conv_w[l], ffn_conv_b[l], wdn, lat_seg, n_lat)
        x_mid = _merge(x_all, mods[2], gates, branch_outs, wbr, wout, ctx_seg, prev_out=x_mid)
        x_new = _ffn(x_mid, mods, norm2_g[l], wup, ffn_conv_w[l], ffn_conv_b[l], wdn, lat_seg, m_tot)
        x_all = _ffn(x_mid, mods, norm2_g[l], wup, ffn_conv_w[l], ffn_conv_b[l], wdn, ctx_seg, m_tot, prev_out=x_new)
    return x_all
```

```python
import functools
import math

import jax
import jax.numpy as jnp
from jax import lax
from jax.experimental import pallas as pl
from jax.experimental.pallas import tpu as pltpu

F32 = jnp.float32
BF16 = jnp.bfloat16

GRID_W = 64
HEAD_DIM = 64
ROPE_BASE = 10000.0
NORM_EPS = 1e-6
SSD_HEADS = 8
SSD_HEAD_DIM = 64
SSD_INNER = 512
SSD_GROUPS = 2
SSD_STATE = 128
SSD_XBC = 1024
GQA_WIDTH = 512
GQA_KV_WIDTH = 128
DIFF_HEADS = 4
DIFF_WIDTH = 512
RG_WIDTH = 512
RG_C = 8.0
BRANCH_WIDTH = 512
N_BRANCHES = 4
IN_SPLITS = (512, 1024, 16, 512, 128, 128, 512, 512, 512, 512, 512)
DT_PAD = 128

VMEM_LIMIT_BYTES = 56 * 1024 * 1024
LANES = 128
F32_SUBLANES = 8
ROW_HALO = 16
CHUNK = 128
TOKEN_TILE = 256
WIDE_TILE = 512


def _cparams(n):
    return pltpu.CompilerParams(dimension_semantics=("parallel",) * n, vmem_limit_bytes=VMEM_LIMIT_BYTES)


def _resident(shape):
    nd = len(shape)
    return pl.BlockSpec(shape, lambda *_: (0,) * nd, pipeline_mode=pl.Buffered(1))


def _sigmoid(x):
    return 0.5 * jnp.tanh(0.5 * x) + 0.5


def _silu(x):
    h = 0.5 * x
    return h + h * jnp.tanh(h)


def _dot(a, b):
    return jnp.dot(a, b, preferred_element_type=F32)


def _row_tile(n, want):
    t = min(want, n)
    while n % t:
        t //= 2
    return t


def _rms_mod(xf, gain, shift, scale):
    ms = jnp.mean(xf * xf, axis=-1, keepdims=True)
    return xf * lax.rsqrt(ms + NORM_EPS) * gain * (1.0 + scale) + shift


def _ada_kernel(c_ref, w_ref, b_ref, o_ref):
    h = _silu(c_ref[...]).astype(BF16)
    o_ref[...] = _dot(h, w_ref[...].astype(BF16)) + b_ref[...]


def _ada(cc, w, b):
    r, d = cc.shape
    n = w.shape[1]
    tn = _row_tile(n, 1536)
    return pl.pallas_call(
        _ada_kernel,
        grid=(n // tn,),
        in_specs=[pl.BlockSpec((r, d), lambda j: (0, 0)),
                  pl.BlockSpec((d, tn), lambda j: (0, j)),
                  pl.BlockSpec((1, tn), lambda j: (0, j))],
        out_specs=pl.BlockSpec((r, tn), lambda j: (0, j)),
        out_shape=jax.ShapeDtypeStruct((r, n), F32),
        compiler_params=_cparams(1),
        name="ada",
    )(cc, w, b.reshape(1, n))


def _proj_kernel(x_ref, sh_ref, sc_ref, g_ref, w_ref, b_ref, pg_ref, ones_ref, cos_ref, sin_ref, *out_refs, specs):
    h = _rms_mod(x_ref[0], g_ref[...], sh_ref[0], sc_ref[0]).astype(BF16)
    tm = h.shape[0]
    lane = lax.broadcasted_iota(jnp.int32, (tm, LANES), 1)
    first_half = (lane % HEAD_DIM) < (HEAD_DIM // 2)
    c0 = 0
    for o_ref, (w, kind, transpose, scale) in zip(out_refs, specs):
        if kind in ("plain", "sigmoid") and not transpose:
            for s in range(0, w, 512):
                e = min(s + 512, w)
                acc = _dot(h, w_ref[:, c0 + s:c0 + e])
                if kind == "sigmoid":
                    acc = _sigmoid(acc + b_ref[:, c0 + s:c0 + e])
                o_ref[0, :, s:e] = acc.astype(o_ref.dtype)
        else:
            y = _dot(h, w_ref[:, c0:c0 + w])
            if kind == "qk":
                ss = _dot((y * y).astype(BF16), ones_ref[0:w, 0:w])
                y = y * lax.rsqrt(ss * (1.0 / HEAD_DIM) + NORM_EPS) * pg_ref[:, c0:c0 + w]
                cos = cos_ref[...]
                sin = sin_ref[...]
                cols = []
                for j in range(w // LANES):
                    yj = y[:, j * LANES:(j + 1) * LANES]
                    partner = jnp.where(first_half, pltpu.roll(yj, LANES - HEAD_DIM // 2, axis=1),
                                        pltpu.roll(yj, HEAD_DIM // 2, axis=1))
                    cols.append(yj * cos + partner * sin)
                y = cols[0] if len(cols) == 1 else jnp.concatenate(cols, axis=1)
                if scale != 1.0:
                    y = y * scale
            if transpose:
                o_ref[0] = y.T.astype(o_ref.dtype)
            else:
                o_ref[0] = y.astype(o_ref.dtype)
        c0 += w


def _in_proj(x3, mods, gain, w_cat, b_cat, post_gain, ones_bd, cos_all, sin_all, specs, dtypes, tm, lat_tiles):
    bsz, m_tot, d = x3.shape
    tpb = m_tot // tm
    ncat = w_cat.shape[1]

    def mod_idx(i):
        return (2 * (i // tpb) + jnp.where(i % tpb >= lat_tiles, 1, 0), 0, 0)

    out_specs, out_shapes = [], []
    for (w, _, transpose, _), dt in zip(specs, dtypes):
        if transpose:
            out_specs.append(pl.BlockSpec((1, w, tm), lambda i: (i // tpb, 0, i % tpb)))
            out_shapes.append(jax.ShapeDtypeStruct((bsz, w, m_tot), dt))
        else:
            out_specs.append(pl.BlockSpec((1, tm, w), lambda i: (i // tpb, i % tpb, 0)))
            out_shapes.append(jax.ShapeDtypeStruct((bsz, m_tot, w), dt))
    kern = functools.partial(_proj_kernel, specs=tuple(specs))
    return pl.pallas_call(
        kern,
        grid=(bsz * tpb,),
        in_specs=[pl.BlockSpec((1, tm, d), lambda i: (i // tpb, i % tpb, 0)),
                  pl.BlockSpec((1, 1, d), mod_idx), pl.BlockSpec((1, 1, d), mod_idx),
                  _resident((1, d)), _resident((d, ncat)), _resident((1, ncat)), _resident((1, ncat)),
                  _resident(ones_bd.shape),
                  pl.BlockSpec((tm, LANES), lambda i: (i % tpb, 0)),
                  pl.BlockSpec((tm, LANES), lambda i: (i % tpb, 0))],
        out_specs=out_specs,
        out_shape=out_shapes,
        compiler_params=_cparams(1),
        name="in_proj",
    )(x3, mods[0], mods[1], gain.reshape(1, d), w_cat, b_cat, post_gain, ones_bd, cos_all, sin_all)


def _conv_window(src, off, n, r0, rows):
    cur = src[0, pl.ds(r0, rows), :].astype(F32)
    lo = pl.multiple_of(jnp.maximum(r0 - ROW_HALO, off), ROW_HALO)
    hi = pl.multiple_of(jnp.minimum(r0 + rows, off + n - ROW_HALO), ROW_HALO)
    prev = src[0, pl.ds(lo, ROW_HALO), :].astype(F32)
    nxt = src[0, pl.ds(hi, ROW_HALO), :].astype(F32)
    prev = jnp.where(r0 > off, prev, 0.0)
    nxt = jnp.where(r0 + rows < off + n, nxt, 0.0)
    return jnp.concatenate([prev, cur, nxt], axis=0)


def _dwconv4(win, rows, cw_ref, cb_ref):
    total = rows + 2 * ROW_HALO
    acc = cb_ref[...] + cw_ref[1:2, :] * win[ROW_HALO:ROW_HALO + rows]
    for k in (0, 2, 3):
        sh = pltpu.roll(win, (1 - k) % total, axis=0)[ROW_HALO:ROW_HALO + rows]
        acc = acc + cw_ref[k:k + 1, :] * sh
    return acc


def _cumsum_rows(a, reverse):
    n = a.shape[0]
    rid = lax.broadcasted_iota(jnp.int32, a.shape, 0)
    k = 1
    while k < n:
        if reverse:
            a = a + jnp.where(rid < n - k, pltpu.roll(a, n - k, axis=0), 0.0)
        else:
            a = a + jnp.where(rid >= k, pltpu.roll(a, k, axis=0), 0.0)
        k *= 2
    return a


def _ssd_kernel(xbc, dtr, z, cw_ref, cb_ref, dtb_ref, aneg_ref, dvec_ref, ng_ref, sel_ref,
                out, xc, ybuf, st_ref, *, segs):
    heads_per_group = SSD_HEADS // SSD_GROUPS
    gw = heads_per_group * SSD_HEAD_DIM
    pair_w = 2 * SSD_HEAD_DIM

    for off, n in segs:
        def conv_body(j, carry, off=off, n=n):
            r0 = pl.multiple_of(off + j * CHUNK, CHUNK)
            win = _conv_window(xbc, off, n, r0, CHUNK)
            xc[pl.ds(r0, CHUNK), :] = _silu(_dwconv4(win, CHUNK, cw_ref, cb_ref)).astype(BF16)
            return carry
        lax.fori_loop(0, n // CHUNK, conv_body, 0)

    rid = lax.broadcasted_iota(jnp.int32, (CHUNK, CHUNK), 0)
    cid = lax.broadcasted_iota(jnp.int32, (CHUNK, CHUNK), 1)
    low_half = cid < SSD_HEAD_DIM

    def chunk(r0, d):
        dt = jax.nn.softplus(dtr[0, pl.ds(r0, CHUNK), :] + dtb_ref[...])
        cs = _cumsum_rows(dt * aneg_ref[...], reverse=(d == 1))
        cst = cs.T
        tot = cs[CHUNK - 1:CHUNK, :] if d == 0 else cs[0:1, :]
        mask = (rid >= cid) if d == 0 else (cid >= rid)
        etot = jnp.exp(tot)
        et_hi = etot.astype(BF16).astype(F32)
        rows8 = lambda v: jnp.broadcast_to(v, (F32_SUBLANES, v.shape[1]))
        stack = jnp.concatenate([dt, dt * jnp.exp(tot - cs), jnp.exp(cs), rows8(et_hi), rows8(etot - et_hi)],
                                axis=0).astype(BF16)
        spread = _dot(stack, sel_ref[d])
        dt_b = spread[0:CHUNK]
        w_b = spread[CHUNK:2 * CHUNK]
        e_b = spread[2 * CHUNK:3 * CHUNK]
        t_b = spread[3 * CHUNK:3 * CHUNK + 1] + spread[3 * CHUNK + F32_SUBLANES:3 * CHUNK + F32_SUBLANES + 1]
        xs = xc[pl.ds(r0, CHUNK), 0:SSD_INNER].astype(F32)
        xdt = xs * dt_b
        xdec = (xs * w_b).astype(BF16)
        ys = []
        for g in range(SSD_GROUPS):
            bg = xc[pl.ds(r0, CHUNK), SSD_INNER + g * SSD_STATE:SSD_INNER + (g + 1) * SSD_STATE]
            cg = xc[pl.ds(r0, CHUNK), SSD_INNER + (SSD_GROUPS + g) * SSD_STATE:
                    SSD_INNER + (SSD_GROUPS + g + 1) * SSD_STATE]
            bgt = bg.astype(F32).T.astype(BF16)
            gmat = _dot(cg, bgt)
            ypairs = []
            for q in range(heads_per_group // 2):
                scs = []
                for hl in (2 * q, 2 * q + 1):
                    col = d * SSD_HEADS + g * heads_per_group + hl
                    diff = cs[:, col:col + 1] - cst[col:col + 1, :]
                    scs.append((gmat * jnp.exp(jnp.where(mask, diff, -jnp.inf))).astype(BF16))
                c0 = g * gw + q * pair_w
                xp = xdt[:, c0:c0 + pair_w]
                rhs = jnp.concatenate([jnp.where(low_half, xp, 0.0).astype(BF16),
                                       jnp.where(low_half, 0.0, xp).astype(BF16)], axis=0)
                ypairs.append(_dot(jnp.concatenate(scs, axis=1), rhs))
            st = st_ref[d * SSD_GROUPS + g]
            yoff = _dot(cg, st.astype(BF16))
            ys.append(jnp.concatenate(ypairs, axis=1) + yoff * e_b[:, g * gw:(g + 1) * gw])
            st_ref[d * SSD_GROUPS + g] = st * t_b[:, g * gw:(g + 1) * gw] + _dot(bgt, xdec[:, g * gw:(g + 1) * gw])
        return ys

    def emit(r0, y_groups):
        zz = _silu(z[0, pl.ds(r0, CHUNK), :].astype(F32))
        gsz = SSD_INNER // SSD_GROUPS
        for g in range(SSD_GROUPS):
            cols = slice(g * gsz, (g + 1) * gsz)
            xs = xc[pl.ds(r0, CHUNK), cols].astype(F32)
            yg = (y_groups[g] + xs * dvec_ref[:, cols]) * zz[:, cols]
            ms = jnp.mean(yg * yg, axis=-1, keepdims=True)
            out[0, pl.ds(r0, CHUNK), cols] = (yg * lax.rsqrt(ms + NORM_EPS) * ng_ref[:, cols]).astype(out.dtype)

    def stash(r0, y_groups):
        for g in range(SSD_GROUPS):
            ybuf[pl.ds(r0, CHUNK), g * gw:(g + 1) * gw] = y_groups[g]

    def with_stash(r0, y_groups):
        return [y_groups[g] + ybuf[pl.ds(r0, CHUNK), g * gw:(g + 1) * gw] for g in range(SSD_GROUPS)]

    st_ref[...] = jnp.zeros_like(st_ref)
    for off, n in segs:
        nch = n // CHUNK

        def rows(k, off=off, nch=nch):
            return (pl.multiple_of(off + k * CHUNK, CHUNK), pl.multiple_of(off + (nch - 1 - k) * CHUNK, CHUNK))

        def first_half(k, carry, rows=rows):
            rf, rb = rows(k)
            stash(rf, chunk(rf, 0))
            stash(rb, chunk(rb, 1))
            return carry
        lax.fori_loop(0, nch // 2, first_half, 0)

        if nch % 2:
            r_mid = off + (nch // 2) * CHUNK
            yf_mid, yb_mid = chunk(r_mid, 0), chunk(r_mid, 1)
            emit(r_mid, [yf_mid[g] + yb_mid[g] for g in range(SSD_GROUPS)])

        def second_half(k, carry, rows=rows):
            rf, rb = rows(k)
            emit(rf, with_stash(rf, chunk(rf, 0)))
            emit(rb, with_stash(rb, chunk(rb, 1)))
            return carry
        lax.fori_loop((nch + 1) // 2, nch, second_half, 0)


def _ssd(xbc, dt_raw, z, conv_w, conv_b, dt_bias, a_log, d_skip, norm_g, segs):
    bsz, m_tot, _ = xbc.shape
    pad = DT_PAD - 2 * SSD_HEADS
    dtb = jnp.pad(dt_bias.astype(F32).reshape(1, -1), ((0, 0), (0, pad)))
    aneg = jnp.pad(-jnp.exp(a_log.astype(F32)).reshape(1, -1), ((0, 0), (0, pad)))
    dvec = jnp.repeat(d_skip.astype(F32), SSD_HEAD_DIM).reshape(1, SSD_INNER)
    col = jnp.arange(DT_PAD)[None, :, None]
    head = (jnp.arange(SSD_INNER) // SSD_HEAD_DIM)[None, None, :]
    sel = (col == jnp.arange(2)[:, None, None] * SSD_HEADS + head).astype(BF16)

    def seq_spec(w, single):
        kw = dict(pipeline_mode=pl.Buffered(1)) if single else {}
        return pl.BlockSpec((1, m_tot, w), lambda b: (b, 0, 0), **kw)

    kern = functools.partial(_ssd_kernel, segs=segs)
    return pl.pallas_call(
        kern,
        grid=(bsz,),
        in_specs=[seq_spec(SSD_XBC, True), seq_spec(DT_PAD, True), seq_spec(SSD_INNER, True),
                  _resident((4, SSD_XBC)), _resident((1, SSD_XBC)), _resident((1, DT_PAD)), _resident((1, DT_PAD)),
                  _resident((1, SSD_INNER)), _resident((1, SSD_INNER)), _resident((2, DT_PAD, SSD_INNER))],
        out_specs=seq_spec(SSD_INNER, False),
        out_shape=jax.ShapeDtypeStruct((bsz, m_tot, SSD_INNER), BF16),
        scratch_shapes=[pltpu.VMEM((m_tot, SSD_XBC), BF16), pltpu.VMEM((m_tot, SSD_INNER), F32),
                        pltpu.VMEM((2 * SSD_GROUPS, SSD_STATE, SSD_INNER // SSD_GROUPS), F32)],
        compiler_params=_cparams(1),
        name="ssd",
    )(xbc, dt_raw, z, conv_w.astype(F32), conv_b.astype(F32).reshape(1, -1), dtb, aneg, dvec,
      norm_g.astype(F32).reshape(1, -1), sel)


def _gelu_tanh(x):
    return 0.5 * x * (1.0 + jnp.tanh(math.sqrt(2.0 / math.pi) * (x + 0.044715 * (x * x * x))))


def _rg_kernel(x, gg, cw_ref, cb_ref, w_ref, b_ref, lam_ref, out, hbuf, a_s, b_s, h_s, hcar, *, segs):
    gw = 2 * RG_WIDTH

    def gates(off, n, r0, d):
        win = _conv_window(x, off, n, r0, CHUNK)
        xr = _dwconv4(win, CHUNK, cw_ref, cb_ref)
        g = _dot(xr.astype(BF16), w_ref[:, d * gw:(d + 1) * gw]) + b_ref[:, d * gw:(d + 1) * gw]
        sp = jax.nn.softplus(-lam_ref[d:d + 1, :])
        log_a = -RG_C * _sigmoid(g[:, :RG_WIDTH]) * sp
        a = jnp.exp(log_a)
        a_s[d] = a
        one_minus_a2 = -jnp.tanh(log_a) * (a * a + 1.0)
        b_s[d] = jnp.sqrt(one_minus_a2) * (_sigmoid(g[:, RG_WIDTH:]) * xr)

    def scan_both():
        blk = 16

        def block(i, carry):
            hf, hb = carry
            f0 = pl.multiple_of(i * blk, blk)
            b0 = pl.multiple_of(CHUNK - blk - i * blk, blk)
            for k in range(blk):
                kb = blk - 1 - k
                hf = a_s[0, pl.ds(f0 + k, 1), :] * hf + b_s[0, pl.ds(f0 + k, 1), :]
                hb = a_s[1, pl.ds(b0 + kb, 1), :] * hb + b_s[1, pl.ds(b0 + kb, 1), :]
                h_s[0, pl.ds(f0 + k, 1), :] = hf
                h_s[1, pl.ds(b0 + kb, 1), :] = hb
            return hf, hb
        hf, hb = lax.fori_loop(0, CHUNK // blk, block, (hcar[0:1, :], hcar[1:2, :]))
        hcar[0:1, :] = hf
        hcar[1:2, :] = hb

    def emit(r0, hsum):
        gate = _gelu_tanh(gg[0, pl.ds(r0, CHUNK), :].astype(F32))
        out[0, pl.ds(r0, CHUNK), :] = (hsum * gate).astype(out.dtype)

    hcar[...] = jnp.zeros_like(hcar)
    for off, n in segs:
        nch = n // CHUNK

        def rows(k, off=off, nch=nch):
            return (pl.multiple_of(off + k * CHUNK, CHUNK), pl.multiple_of(off + (nch - 1 - k) * CHUNK, CHUNK))

        def first_half(k, carry, off=off, n=n, rows=rows):
            rf, rb = rows(k)
            gates(off, n, rf, 0)
            gates(off, n, rb, 1)
            scan_both()
            hbuf[pl.ds(rf, CHUNK), :] = h_s[0]
            hbuf[pl.ds(rb, CHUNK), :] = h_s[1]
            return carry
        lax.fori_loop(0, nch // 2, first_half, 0)

        if nch % 2:
            r_mid = off + (nch // 2) * CHUNK
            gates(off, n, r_mid, 0)
            gates(off, n, r_mid, 1)
            scan_both()
            emit(r_mid, h_s[0] + h_s[1])

        def second_half(k, carry, off=off, n=n, rows=rows):
            rf, rb = rows(k)
            gates(off, n, rf, 0)
            gates(off, n, rb, 1)
            scan_both()
            emit(rf, h_s[0] + hbuf[pl.ds(rf, CHUNK), :])
            emit(rb, hbuf[pl.ds(rb, CHUNK), :] + h_s[1])
            return carry
        lax.fori_loop((nch + 1) // 2, nch, second_half, 0)


def _block_diag(w):
    k, c, e = w.shape
    return jnp.einsum("kce,kj->kcje", w, jnp.eye(k, dtype=w.dtype)).reshape(k * c, k * e)


def _rglru(rgx, rgg, conv_w, conv_b, wa, ba, wx, bx, lam, segs):
    bsz, m_tot, _ = rgx.shape
    w_cat = jnp.concatenate([_block_diag(wa[0]), _block_diag(wx[0]), _block_diag(wa[1]), _block_diag(wx[1])],
                            axis=1).astype(BF16)
    b_cat = jnp.concatenate([ba[0], bx[0], ba[1], bx[1]]).astype(F32).reshape(1, -1)
    seq_spec = pl.BlockSpec((1, m_tot, RG_WIDTH), lambda b: (b, 0, 0))
    kern = functools.partial(_rg_kernel, segs=segs)
    return pl.pallas_call(
        kern,
        grid=(bsz,),
        in_specs=[seq_spec, seq_spec,
                  _resident((4, RG_WIDTH)), _resident((1, RG_WIDTH)), _resident((RG_WIDTH, 4 * RG_WIDTH)),
                  _resident((1, 4 * RG_WIDTH)), _resident((2, RG_WIDTH))],
        out_specs=seq_spec,
        out_shape=jax.ShapeDtypeStruct((bsz, m_tot, RG_WIDTH), BF16),
        scratch_shapes=[pltpu.VMEM((m_tot, RG_WIDTH), F32),
                        pltpu.VMEM((2, CHUNK, RG_WIDTH), F32), pltpu.VMEM((2, CHUNK, RG_WIDTH), F32),
                        pltpu.VMEM((2, CHUNK, RG_WIDTH), F32), pltpu.VMEM((2, RG_WIDTH), F32)],
        compiler_params=_cparams(1),
        name="rglru",
    )(rgx, rgg, conv_w.astype(F32), conv_b.astype(F32).reshape(1, -1), w_cat, b_cat, lam.astype(F32))


KEY_CHUNK = 256
ATTN_COLS = 1024
ONES_ROWS = 16


def _attn_core(qt_ext, k_ref, vt_ref, key_off, n_keys, tk):
    chunks = [(key_off + o, min(tk, n_keys - o)) for o in range(0, n_keys, tk)]
    n_chunks = len(chunks)

    def scores(j):
        off, size = chunks[min(j, n_chunks - 1)]
        s = _dot(k_ref[0, off:off + size, :], qt_ext)
        return s, jnp.max(s, axis=0, keepdims=True)

    def softmax(s, m_old, cmax):
        m_new = jnp.maximum(m_old, cmax)
        return jnp.exp2((s - m_new).astype(BF16)), m_new, jnp.exp2(m_old - m_new)

    def accumulate(j, acc, p, alpha):
        off, size = chunks[j]
        vt = jnp.concatenate([vt_ref[0, :, off:off + size], jnp.ones((ONES_ROWS, size), BF16)], axis=0)
        pv = _dot(vt, p)
        return pv if acc is None else alpha * acc + pv

    s_cur, c_cur = scores(0)
    s_nxt, c_nxt = scores(1)
    p_prev, m, a_prev = softmax(s_cur, jnp.full(c_cur.shape, -jnp.inf, F32), c_cur)
    acc = None
    for j in range(1, n_chunks):
        acc = accumulate(j - 1, acc, p_prev, a_prev)
        s_cur, c_cur = s_nxt, c_nxt
        s_nxt, c_nxt = scores(j + 1)
        p_prev, m, a_prev = softmax(s_cur, m, c_cur)
    return accumulate(n_chunks - 1, acc, p_prev, a_prev)


def _gqa_kernel(qt_ref, k_ref, vt_ref, *rest, key_off, n_keys, tk, tq, rep, kv_heads):
    o_ref = rest[-1]
    qt = qt_ref[0]
    qcat = jnp.concatenate([qt[r * HEAD_DIM:(r + 1) * HEAD_DIM, :] for r in range(rep)], axis=1)
    g = pl.program_id(1)
    zero = jnp.zeros_like(qcat)
    qt_ext = jnp.concatenate([jnp.where(g == j, qcat, zero) for j in range(kv_heads)], axis=0)
    acc = _attn_core(qt_ext, k_ref, vt_ref, key_off, n_keys, tk)
    o = acc[0:HEAD_DIM, :] / acc[HEAD_DIM:HEAD_DIM + 1, :]
    o2 = jnp.concatenate([o[:, r * tq:(r + 1) * tq] for r in range(rep)], axis=0)
    o_ref[0] = o2.T.astype(o_ref.dtype)


def _alias_args(prev_out, n_inputs):
    if prev_out is None:
        return [], [], {}
    return [pl.BlockSpec(memory_space=pl.ANY)], [prev_out], {n_inputs: 0}


def _gqa_attention(qt, k_all, vt_all, *, q_off, n_q, key_off, n_keys, tk, out_rows, prev_out=None):
    bsz, _, m_tot = qt.shape
    kv_heads = GQA_KV_WIDTH // HEAD_DIM
    rep = GQA_WIDTH // GQA_KV_WIDTH
    tq = _row_tile(math.gcd(n_q, q_off) if q_off else n_q, ATTN_COLS // rep)
    qb0 = q_off // tq
    qw = rep * HEAD_DIM
    alias_specs, alias_args, aliases = _alias_args(prev_out, 3)
    kern = functools.partial(_gqa_kernel, key_off=key_off, n_keys=n_keys, tk=tk, tq=tq, rep=rep, kv_heads=kv_heads)
    return pl.pallas_call(
        kern, grid=(bsz, kv_heads, n_q // tq),
        in_specs=[pl.BlockSpec((1, qw, tq), lambda b, g, i: (b, g, qb0 + i)),
                  pl.BlockSpec((1, m_tot, GQA_KV_WIDTH), lambda b, g, i: (b, 0, 0)),
                  pl.BlockSpec((1, HEAD_DIM, m_tot), lambda b, g, i: (b, g, 0))] + alias_specs,
        out_specs=pl.BlockSpec((1, tq, qw), lambda b, g, i: (b, qb0 + i, g)),
        out_shape=jax.ShapeDtypeStruct((bsz, out_rows, GQA_WIDTH), BF16),
        input_output_aliases=aliases,
        compiler_params=_cparams(3), name="gqa_attn")(qt, k_all, vt_all, *alias_args)


def _diff_kernel(qt_ref, k_ref, vt_ref, lam_ref, sg_ref, *rest, key_off, n_keys, tk, tq, lambda_init):
    o_ref = rest[-1]
    qt = qt_ref[0]
    z = jnp.zeros((HEAD_DIM, tq), qt.dtype)
    qt_ext = jnp.concatenate([jnp.concatenate([qt[:HEAD_DIM], z], axis=1),
                              jnp.concatenate([z, qt[HEAD_DIM:]], axis=1)], axis=0)
    acc = _attn_core(qt_ext, k_ref, vt_ref, key_off, n_keys, tk)
    vw = 2 * HEAD_DIM
    o = acc[0:vw, :] / acc[vw:vw + 1, :]
    lp = lam_ref[...]
    lam = (jnp.exp(jnp.sum(lp[0:1] * lp[1:2], axis=-1, keepdims=True))
           - jnp.exp(jnp.sum(lp[2:3] * lp[3:4], axis=-1, keepdims=True)) + lambda_init)
    o = (o[:, :tq] - lam * o[:, tq:]).T
    ms = jnp.mean(o * o, axis=-1, keepdims=True)
    o = o * lax.rsqrt(ms + NORM_EPS) * sg_ref[...] * (1.0 - lambda_init)
    o_ref[0] = o.astype(o_ref.dtype)


def _diff_attention(qt, k_all, vt_all, lam_p, subln_g, lambda_init, *, q_off, n_q, key_off, n_keys, tk,
                    out_rows, prev_out=None):
    bsz, _, m_tot = qt.shape
    tq = _row_tile(math.gcd(n_q, q_off) if q_off else n_q, ATTN_COLS // 2)
    qb0 = q_off // tq
    vw = 2 * HEAD_DIM
    alias_specs, alias_args, aliases = _alias_args(prev_out, 5)
    kern = functools.partial(_diff_kernel, key_off=key_off, n_keys=n_keys, tk=tk, tq=tq, lambda_init=lambda_init)
    return pl.pallas_call(
        kern, grid=(bsz, DIFF_HEADS, n_q // tq),
        in_specs=[pl.BlockSpec((1, vw, tq), lambda b, h, i: (b, h, qb0 + i)),
                  pl.BlockSpec((1, m_tot, vw), lambda b, h, i: (b, 0, h)),
                  pl.BlockSpec((1, vw, m_tot), lambda b, h, i: (b, h, 0)),
                  _resident((4, HEAD_DIM)), _resident((1, vw))] + alias_specs,
        out_specs=pl.BlockSpec((1, tq, vw), lambda b, h, i: (b, qb0 + i, h)),
        out_shape=jax.ShapeDtypeStruct((bsz, out_rows, DIFF_WIDTH), BF16),
        input_output_aliases=aliases,
        compiler_params=_cparams(3), name="diff_attn")(qt, k_all, vt_all, lam_p.astype(F32),
                                                       subln_g.astype(F32).reshape(1, vw), *alias_args)


def _merge_kernel(x_ref, g1_ref, gates_ref, o0, o1, o2, o3, wbr_ref, wout_ref, *rest):
    out_ref = rest[-1]
    d = x_ref.shape[2]
    m = None
    for k, o in enumerate((o0, o1, o2, o3)):
        t = gates_ref[0, :, k * d:(k + 1) * d].astype(F32) * _dot(o[0], wbr_ref[k])
        m = t if m is None else m + t
    y = _dot(m.astype(BF16), wout_ref[...])
    out_ref[0] = x_ref[0] + g1_ref[0] * y


def _seg_tiles(seg):
    off, rows, _ = seg
    tm = _row_tile(math.gcd(rows, off) if off else rows, WIDE_TILE)
    return tm, off // tm, rows // tm


def _merge(x3, g1, gates, outs, w_br, w_out, seg):
    bsz, m_tot, d = x3.shape
    tm, t0, n_tiles = _seg_tiles(seg)
    is_ctx = int(seg[2])
    row = lambda w: pl.BlockSpec((1, tm, w), lambda b, i: (b, t0 + i, 0))
    return pl.pallas_call(
        _merge_kernel,
        grid=(bsz, n_tiles),
        in_specs=[row(d), pl.BlockSpec((1, 1, d), lambda b, i: (2 * b + is_ctx, 0, 0)),
                  row(N_BRANCHES * d), row(BRANCH_WIDTH), row(BRANCH_WIDTH), row(BRANCH_WIDTH), row(BRANCH_WIDTH),
                  _resident((N_BRANCHES, BRANCH_WIDTH, d)), _resident((d, d))],
        out_specs=row(d),
        out_shape=jax.ShapeDtypeStruct((bsz, m_tot, d), F32),
        input_output_aliases={0: 0},
        compiler_params=_cparams(2),
        name="merge",
    )(x3, g1, gates, *outs, w_br, w_out)


FFN_COL_CHUNK = 256


def _ffn_kernel(x_ref, xp_ref, xn_ref, sh_ref, sc_ref, g2_ref, gn_ref, wu_ref, cw_ref, cb_ref, wd_ref,
                *rest, n_tiles, d_ff):
    out_ref, act_ref = rest[-2:]
    tm = x_ref.shape[1]
    i = pl.program_id(1)
    has_prev = i != 0
    has_next = i != n_tiles - 1
    gain, shift, scale = gn_ref[...], sh_ref[0], sc_ref[0]
    xf = x_ref[0]
    h_prev = jnp.where(has_prev, _rms_mod(xp_ref[0], gain, shift, scale), 0.0)
    h_next = jnp.where(has_next, _rms_mod(xn_ref[0], gain, shift, scale), 0.0)
    h = jnp.concatenate([h_prev, _rms_mod(xf, gain, shift, scale), h_next], axis=0).astype(BF16)
    rows = tm + 2 * F32_SUBLANES

    def conv(c0):
        u = _dot(h, wu_ref[:, c0:c0 + FFN_COL_CHUNK])
        w = cw_ref[:, c0:c0 + FFN_COL_CHUNK]
        y = (w[0:1] * pltpu.roll(u, 1, axis=0) + w[1:2] * u + w[2:3] * pltpu.roll(u, rows - 1, axis=0)
             + cb_ref[:, c0:c0 + FFN_COL_CHUNK])
        return y[F32_SUBLANES:F32_SUBLANES + tm]

    for c in range(d_ff // FFN_COL_CHUNK):
        c0 = c * FFN_COL_CHUNK
        act_ref[:, c0:c0 + FFN_COL_CHUNK] = (_silu(conv(c0)) * conv(d_ff + c0)).astype(BF16)
    out_ref[0] = xf + g2_ref[0] * _dot(act_ref[...], wd_ref[...])


def _ffn(x3, mods, gain, w_up, conv_w, conv_b, w_down, seg, out_rows, prev_out=None):
    bsz, m_tot, d = x3.shape
    d_ff = w_down.shape[0]
    tm, t0, n_tiles = _seg_tiles(seg)
    is_ctx = int(seg[2])
    hb = tm // F32_SUBLANES
    n_halo = m_tot // F32_SUBLANES
    mod_spec = pl.BlockSpec((1, 1, d), lambda b, i: (2 * b + is_ctx, 0, 0))
    alias_specs, alias_args, aliases = _alias_args(prev_out, 11)
    kern = functools.partial(_ffn_kernel, n_tiles=n_tiles, d_ff=d_ff)
    return pl.pallas_call(
        kern,
        grid=(bsz, n_tiles),
        in_specs=[pl.BlockSpec((1, tm, d), lambda b, i: (b, t0 + i, 0)),
                  pl.BlockSpec((1, F32_SUBLANES, d), lambda b, i: (b, jnp.maximum((t0 + i) * hb - 1, 0), 0)),
                  pl.BlockSpec((1, F32_SUBLANES, d),
                               lambda b, i: (b, jnp.minimum((t0 + i + 1) * hb, n_halo - 1), 0)),
                  mod_spec, mod_spec, mod_spec, _resident((1, d)), _resident((d, 2 * d_ff)),
                  _resident((3, 2 * d_ff)), _resident((1, 2 * d_ff)), _resident((d_ff, d))] + alias_specs,
        out_specs=pl.BlockSpec((1, tm, d), lambda b, i: (b, t0 + i, 0)),
        out_shape=jax.ShapeDtypeStruct((bsz, out_rows, d), F32),
        scratch_shapes=[pltpu.VMEM((tm, d_ff), BF16)],
        input_output_aliases=aliases,
        compiler_params=_cparams(2),
        name="conv_ffn",
    )(x3, x3, x3, mods[3], mods[4], mods[5], gain.reshape(1, d), w_up, conv_w.astype(F32),
      conv_b.astype(F32).reshape(1, -1), w_down, *alias_args)


def _rope_tables(n_lat, m_ctx):
    t = jnp.arange(n_lat)
    row = (t // GRID_W).astype(F32)
    col = (t % GRID_W).astype(F32)
    n_freq = HEAD_DIM // 4
    inv = ROPE_BASE ** (-jnp.arange(n_freq, dtype=F32) / n_freq)
    ang = jnp.concatenate([row[:, None] * inv, col[:, None] * inv], axis=-1)
    cos, sin = jnp.cos(ang), jnp.sin(ang)
    reps = LANES // HEAD_DIM
    cos_l = jnp.tile(jnp.concatenate([cos, cos], axis=-1), (1, reps))
    sin_l = jnp.tile(jnp.concatenate([-sin, sin], axis=-1), (1, reps))
    return (jnp.concatenate([cos_l, jnp.ones((m_ctx, LANES), F32)], axis=0),
            jnp.concatenate([sin_l, jnp.zeros((m_ctx, LANES), F32)], axis=0))


def kernel(x, c, ctx, c_ctx, w_ada, b_ada, norm1_g, norm2_g, w_in, ssd_conv_w, ssd_conv_b, ssd_dt_bias, ssd_a_log, ssd_d, ssd_norm_g, gqa_qnorm_g, gqa_knorm_g, diff_qnorm_g, diff_knorm_g, diff_lambda, diff_subln_g, rg_conv_w, rg_conv_b, rg_wa, rg_ba, rg_wx, rg_bx, rg_lambda, w_gate, b_gate, w_br, w_out, w_up, ffn_conv_w, ffn_conv_b, w_down):
    bsz, n_lat, d = x.shape
    m_ctx = ctx.shape[1]
    m_tot = n_lat + m_ctx
    depth = w_in.shape[0]
    tm = math.gcd(math.gcd(m_ctx, n_lat), TOKEN_TILE)
    tpb, lat_tiles = m_tot // tm, n_lat // tm
    tk = min(KEY_CHUNK, m_tot)
    segs = ((n_lat, m_ctx), (0, n_lat))
    cos_all, sin_all = _rope_tables(n_lat, m_ctx)
    qscale = HEAD_DIM ** -0.5 * math.log2(math.e)
    seg = lax.broadcasted_iota(jnp.int32, (GQA_WIDTH, GQA_WIDTH), 0) // HEAD_DIM
    ones_bd = (seg == seg.T).astype(BF16)

    x_all = jnp.concatenate([x, ctx], axis=1)
    n_c = bsz + 1
    n_c_pad = -(-n_c // F32_SUBLANES) * F32_SUBLANES
    cc = jnp.pad(jnp.concatenate([c, c_ctx[None, :]], axis=0), ((0, n_c_pad - n_c), (0, 0)))
    dt_col = IN_SPLITS[0] + IN_SPLITS[1]

    specs = ((512, "plain", False, 1.0), (1024, "plain", False, 1.0), (DT_PAD, "plain", False, 1.0),
             (512, "qk", True, qscale), (128, "qk", False, 1.0), (128, "vt", True, 1.0),
             (512, "qk", True, qscale), (512, "qk", False, 1.0), (512, "vt", True, 1.0),
             (512, "plain", False, 1.0), (512, "plain", False, 1.0), (N_BRANCHES * d, "sigmoid", False, 1.0))
    dtypes = (BF16, BF16, F32) + (BF16,) * 9
    in_width = sum(s[0] for s in specs[:-1])

    for l in range(depth):
        last = l == depth - 1
        lambda_init = 0.8 - 0.6 * math.exp(-0.3 * l)

        mod = _ada(cc, w_ada[l], b_ada[l])
        mod_pairs = jnp.stack([mod[:bsz], jnp.broadcast_to(mod[bsz:bsz + 1], (bsz, 6 * d))], axis=1)
        mods = [mod_pairs[:, :, k * d:(k + 1) * d].reshape(2 * bsz, 1, d) for k in range(6)]

        wi = w_in[l]
        w_cat = jnp.concatenate(
            [wi[:, :dt_col + 16], jnp.zeros((d, DT_PAD - 16), wi.dtype), wi[:, dt_col + 16:],
             jnp.transpose(w_gate[l], (1, 0, 2)).reshape(d, N_BRANCHES * d)], axis=1).astype(BF16)
        b_cat = jnp.concatenate([jnp.zeros((in_width,), F32), b_gate[l].reshape(-1).astype(F32)]).reshape(1, -1)
        tile_g = lambda g, w: jnp.tile(g.astype(F32), w // HEAD_DIM)
        ones = lambda w: jnp.ones((w,), F32)
        post_gain = jnp.concatenate(
            [ones(512 + 1024 + DT_PAD), tile_g(gqa_qnorm_g[l], 512), tile_g(gqa_knorm_g[l], 128), ones(128),
             tile_g(diff_qnorm_g[l], 512), tile_g(diff_knorm_g[l], 512), ones(512 + 512 + 512 + N_BRANCHES * d)]
        ).reshape(1, -1)

        (z, xbc, dt_raw, gqt, gk, gvt, dqt, dk, dvt, rgg, rgx, gates) = _in_proj(
            x_all, mods, norm1_g[l], w_cat, b_cat, post_gain, ones_bd, cos_all, sin_all, specs, dtypes, tm, lat_tiles)

        ssd_o = _ssd(xbc, dt_raw, z, ssd_conv_w[l], ssd_conv_b[l], ssd_dt_bias[l], ssd_a_log[l], ssd_d[l],
                     ssd_norm_g[l], segs)
        rg_o = _rglru(rgx, rgg, rg_conv_w[l], rg_conv_b[l], rg_wa[l], rg_ba[l], rg_wx[l], rg_bx[l], rg_lambda[l],
                      segs)
        out_rows = n_lat if last else m_tot
        zeros_like_out = lambda w, dt: None if last else jnp.zeros((bsz, m_tot, w), dt)
        lat_q = dict(q_off=0, n_q=n_lat, key_off=0, n_keys=m_tot, tk=tk, out_rows=out_rows)
        ctx_q = dict(q_off=n_lat, n_q=m_ctx, key_off=n_lat, n_keys=m_ctx, tk=tk, out_rows=out_rows)
        gqa_o = _gqa_attention(gqt, gk, gvt, prev_out=zeros_like_out(GQA_WIDTH, BF16), **lat_q)
        diff_o = _diff_attention(dqt, dk, dvt, diff_lambda[l], diff_subln_g[l], lambda_init,
                                 prev_out=zeros_like_out(DIFF_WIDTH, BF16), **lat_q)
        if not last:
            gqa_o = _gqa_attention(gqt, gk, gvt, prev_out=gqa_o, **ctx_q)
            diff_o = _diff_attention(dqt, dk, dvt, diff_lambda[l], diff_subln_g[l], lambda_init, prev_out=diff_o,
                                     **ctx_q)

        branch_outs = (ssd_o, gqa_o, diff_o, rg_o)
        wbr, wout = w_br[l].astype(BF16), w_out[l].astype(BF16)
        wup, wdn = w_up[l].astype(BF16), w_down[l].astype(BF16)
        lat_seg, ctx_seg = (0, n_lat, False), (n_lat, m_ctx, True)
        x_mid = _merge(x_all, mods[2], gates, branch_outs, wbr, wout, lat_seg)
        if last:
            return _ffn(x_mid, mods, norm2_g[l], wup, ffn_conv_w[l], ffn_conv_b[l], wdn, lat_seg, n_lat)
        x_mid = _merge(x_mid, mods[2], gates, branch_outs, wbr, wout, ctx_seg)
        x_new = _ffn(x_mid, mods, norm2_g[l], wup, ffn_conv_w[l], ffn_conv_b[l], wdn, lat_seg, m_tot,
                     prev_out=zeros_like_out(d, F32))
        x_all = _ffn(x_mid, mods, norm2_g[l], wup, ffn_conv_w[l], ffn_conv_b[l], wdn, ctx_seg, m_tot, prev_out=x_new)
    return x_all
```
